```python
import math
import jax, jax.numpy as jnp
from jax import lax
import numpy as np

D_MODEL = 2048
BATCH = 4
SEQ = 4096
DEPTH = 2

GRID_W = 64
CTX_LEN = 256

GDN_QK_HEADS = 16
GDN_V_HEADS = 32
GDN_DK = 128
GDN_DV = 128
GDN_CONV = 5
GDN_CHUNK = 64
GDN_Q = GDN_QK_HEADS * GDN_DK
GDN_K = GDN_QK_HEADS * GDN_DK
GDN_V = GDN_V_HEADS * GDN_DV
GDN_QKV = GDN_Q + GDN_K + GDN_V
GDN_IN = GDN_QKV + GDN_V + 4 * GDN_V_HEADS

DIFF_HEADS = 8
DIFF_DH = 128
DIFF_DV = 2 * DIFF_DH
DIFF_Q = DIFF_HEADS * 2 * DIFF_DH
DIFF_V = DIFF_HEADS * DIFF_DV
DIFF_IN = 2 * DIFF_Q + DIFF_V
Q_BLOCK = 128
ROPE_BASE = 10000.0
ROPE_AXIS_DIM = DIFF_DH // 2

D_FF = 5632
N_EXPERTS = 8
TOP_K = 2
D_FF_EXPERT = 7168

LN_EPS = 1e-5
RMS_EPS = 1e-6

kernel_name = 'hybrid_gdn_diffattn_moe_dit_block'


def layer_norm(x, w, b):
    xf = x.astype(jnp.float32)
    mu = jnp.mean(xf, axis=-1, keepdims=True)
    var = jnp.mean(jnp.square(xf - mu), axis=-1, keepdims=True)
    return ((xf - mu) * lax.rsqrt(var + LN_EPS) * w + b).astype(x.dtype)


def rms_norm(x, w):
    xf = x.astype(jnp.float32)
    return xf * lax.rsqrt(jnp.mean(jnp.square(xf), axis=-1, keepdims=True) + RMS_EPS) * w


def l2_normalize(x):
    xf = x.astype(jnp.float32)
    return xf * lax.rsqrt(jnp.sum(xf * xf, axis=-1, keepdims=True) + 1e-6)


def modulation(cond, w, b):
    return jnp.split(jax.nn.silu(cond) @ w + b, 6, axis=-1)


def centred_depthwise_conv(x, w):
    pad = (GDN_CONV - 1) // 2
    return lax.conv_general_dilated(
        x, w[:, None, :].astype(x.dtype), window_strides=(1,), padding=[(pad, pad)],
        dimension_numbers=('NWC', 'WIO', 'NWC'), feature_group_count=x.shape[-1])


def gated_delta_chunked(q, k, v, g, beta, s0):
    B, H, T, dk = q.shape
    dv = v.shape[-1]
    C = GDN_CHUNK
    n = T // C
    q = q.reshape(B, H, n, C, dk)
    k = k.reshape(B, H, n, C, dk)
    v = v.reshape(B, H, n, C, dv)
    g = jnp.cumsum(g.reshape(B, H, n, C), axis=-1)
    beta = beta.reshape(B, H, n, C)
    idx = jnp.arange(C)
    lower = idx[:, None] >= idx[None, :]
    strict = idx[:, None] > idx[None, :]
    gdiff = g[..., :, None] - g[..., None, :]
    decay = jnp.where(lower, jnp.exp(jnp.where(lower, gdiff, 0.0)), 0.0)
    k_beta = k * beta[..., None]
    a = jnp.where(strict, jnp.einsum('bhncd,bhnsd->bhncs', k_beta, k) * decay, 0.0)
    eye = jnp.eye(C, dtype=jnp.float32)
    t_inv = lax.linalg.triangular_solve(a + eye, jnp.broadcast_to(eye, a.shape),
                                        left_side=True, lower=True, unit_diagonal=True)
    u = jnp.einsum('bhncs,bhnsv->bhncv', t_inv, v * beta[..., None])
    w = jnp.einsum('bhncs,bhnsd->bhncd', t_inv, k_beta * jnp.exp(g)[..., None])
    qk = jnp.where(lower, jnp.einsum('bhncd,bhnsd->bhncs', q, k) * decay, 0.0)
    q_g = q * jnp.exp(g)[..., None]
    g_last = g[..., -1]
    k_tail = k * jnp.exp(g_last[..., None] - g)[..., None]

    def step(state, inp):
        qk_i, u_i, w_i, qg_i, kt_i, gl_i = inp
        v_new = u_i - jnp.einsum('bhcd,bhdv->bhcv', w_i, state)
        o_i = jnp.einsum('bhcd,bhdv->bhcv', qg_i, state) + jnp.einsum('bhcs,bhsv->bhcv', qk_i, v_new)
        state = state * jnp.exp(gl_i)[..., None, None] + jnp.einsum('bhcd,bhcv->bhdv', kt_i, v_new)
        return state, o_i

    xs = tuple(jnp.moveaxis(t, 2, 0) for t in (qk, u, w, q_g, k_tail, g_last))
    s_final, o = lax.scan(step, s0.astype(jnp.float32), xs)
    return jnp.moveaxis(o, 0, 2).reshape(B, H, T, dv), s_final


def _flip(t):
    return jnp.flip(t, axis=2)


def _gdn_project(t, w_in, conv_w, a_log, dt_bias):
    B, T = t.shape[:2]
    p = t @ w_in
    qkv, z, ab = jnp.split(p, [GDN_QKV, GDN_QKV + GDN_V], axis=-1)
    qkv = jax.nn.silu(centred_depthwise_conv(qkv, conv_w))
    q, k, v = jnp.split(qkv, [GDN_Q, GDN_Q + GDN_K], axis=-1)
    rep = GDN_V_HEADS // GDN_QK_HEADS
    q = jnp.repeat(l2_normalize(q.reshape(B, T, GDN_QK_HEADS, GDN_DK)), rep, axis=2) * (GDN_DK ** -0.5)
    k = jnp.repeat(l2_normalize(k.reshape(B, T, GDN_QK_HEADS, GDN_DK)), rep, axis=2)
    v = v.reshape(B, T, GDN_V_HEADS, GDN_DV).astype(jnp.float32)
    b_f, b_b, a_f, a_b = jnp.split(ab.astype(jnp.float32), 4, axis=-1)
    betas = (jax.nn.sigmoid(b_f), jax.nn.sigmoid(b_b))
    gs = (-jnp.exp(a_log[0].astype(jnp.float32)) * jax.nn.softplus(a_f + dt_bias[0]),
          -jnp.exp(a_log[1].astype(jnp.float32)) * jax.nn.softplus(a_b + dt_bias[1]))
    bhtd = lambda t4: jnp.transpose(t4, (0, 2, 1, 3))
    bht = lambda t3: jnp.transpose(t3, (0, 2, 1))
    return (bhtd(q), bhtd(k), bhtd(v), z.reshape(B, T, GDN_V_HEADS, GDN_DV),
            (bht(betas[0]), bht(betas[1])), (bht(gs[0]), bht(gs[1])))


def _gdn_out(o, z, norm_w, w_out):
    B, H, T, dv = o.shape
    o = rms_norm(jnp.transpose(o, (0, 2, 1, 3)), norm_w) * jax.nn.silu(z.astype(jnp.float32))
    return o.reshape(B, T, H * dv) @ w_out


def gdn_mixer(h, hc, w_in, conv_w, a_log, dt_bias, norm_w, w_out, last):
    qc, kc, vc, zc, bc, gc = _gdn_project(hc, w_in, conv_w, a_log, dt_bias)
    ql, kl, vl, zl, bl, gl = _gdn_project(h, w_in, conv_w, a_log, dt_bias)
    s0 = jnp.zeros((h.shape[0], GDN_V_HEADS, GDN_DK, GDN_DV), jnp.float32)
    oc_f, sc_f = gated_delta_chunked(qc, kc, vc, gc[0], bc[0], s0)
    ol_f, _ = gated_delta_chunked(ql, kl, vl, gl[0], bl[0], sc_f)
    oc_b, sc_b = gated_delta_chunked(_flip(qc), _flip(kc), _flip(vc), _flip(gc[1]), _flip(bc[1]), s0)
    ol_b, _ = gated_delta_chunked(_flip(ql), _flip(kl), _flip(vl), _flip(gl[1]), _flip(bl[1]), sc_b)
    y = _gdn_out(ol_f + _flip(ol_b), zl, norm_w, w_out).astype(h.dtype)
    if last:
        return y, None
    yc = _gdn_out(oc_f + _flip(oc_b), zc, norm_w, w_out).astype(hc.dtype)
    return y, yc


def axial_rope_tables(n_tokens):
    rows = n_tokens // GRID_W
    row = jnp.repeat(jnp.arange(rows, dtype=jnp.float32), GRID_W)
    col = jnp.tile(jnp.arange(GRID_W, dtype=jnp.float32), rows)
    inv_freq = 1.0 / (ROPE_BASE ** (jnp.arange(0, ROPE_AXIS_DIM, 2, dtype=jnp.float32) / ROPE_AXIS_DIM))
    ang_r = row[:, None] * inv_freq
    ang_c = col[:, None] * inv_freq
    return jnp.cos(ang_r), jnp.sin(ang_r), jnp.cos(ang_c), jnp.sin(ang_c)


def _rotate(xa, cos, sin):
    half = xa.shape[-1] // 2
    x1, x2 = xa[..., :half], xa[..., half:]
    cos = cos[None, :, None, None, :]
    sin = sin[None, :, None, None, :]
    return jnp.concatenate([x1 * cos - x2 * sin, x2 * cos + x1 * sin], axis=-1)


def apply_axial_rope(x, tables):
    cr, sr, cc, sc = tables
    xf = x.astype(jnp.float32)
    out = jnp.concatenate([_rotate(xf[..., :ROPE_AXIS_DIM], cr, sr),
                           _rotate(xf[..., ROPE_AXIS_DIM:], cc, sc)], axis=-1)
    return out.astype(x.dtype)


def _diff_out(o, norm_w, w_out, lam_init):
    B, T = o.shape[:2]
    o = rms_norm(o, norm_w) * (1.0 - lam_init)
    return o.reshape(B, T, DIFF_HEADS * DIFF_DV) @ w_out


def diff_mixer(h, hc, w_in, lam_p, norm_w, w_out, layer_idx, last):
    lam_init = 0.8 - 0.6 * math.exp(-0.3 * layer_idx)
    lp = lam_p.astype(jnp.float32)
    lam = jnp.exp(jnp.sum(lp[0] * lp[1])) - jnp.exp(jnp.sum(lp[2] * lp[3])) + lam_init

    def project(t):
        B, T = t.shape[:2]
        q, k, v = jnp.split(t @ w_in, [DIFF_Q, 2 * DIFF_Q], axis=-1)
        return (q.reshape(B, T, DIFF_HEADS, 2, DIFF_DH), k.reshape(B, T, DIFF_HEADS, 2, DIFF_DH),
                v.reshape(B, T, DIFF_HEADS, DIFF_DV))

    ql, kl, vl = project(h)
    qc, kc, vc = project(hc)
    tables = axial_rope_tables(h.shape[1])
    ql = apply_axial_rope(ql, tables)
    kl = apply_axial_rope(kl, tables)
    qk_layout = lambda t: jnp.transpose(t, (0, 2, 3, 1, 4))
    v_layout = lambda t: jnp.transpose(t, (0, 2, 1, 3))
    kc_t, vc_t = qk_layout(kc), v_layout(vc)
    k_all = jnp.concatenate([kc_t, qk_layout(kl)], axis=3)
    v_all = jnp.concatenate([vc_t, v_layout(vl)], axis=2)
    scale = DIFF_DH ** -0.5

    def attend(qb, keys, vals):
        s = jnp.einsum('bhiqd,bhikd->bhiqk', qb, keys, preferred_element_type=jnp.float32) * scale
        p = jax.nn.softmax(s, axis=-1)
        a = p[:, :, 0] - lam * p[:, :, 1]
        return jnp.einsum('bhqk,bhkv->bhqv', a, vals.astype(jnp.float32))

    B, T = h.shape[:2]
    nb = T // Q_BLOCK
    qb = jnp.moveaxis(qk_layout(ql).reshape(B, DIFF_HEADS, 2, nb, Q_BLOCK, DIFF_DH), 3, 0)
    ol = lax.map(lambda q_blk: attend(q_blk, k_all, v_all), qb)
    ol = jnp.transpose(ol, (1, 0, 3, 2, 4)).reshape(B, T, DIFF_HEADS, DIFF_DV)
    y = _diff_out(ol, norm_w, w_out, lam_init).astype(h.dtype)
    if last:
        return y, None
    oc = jnp.transpose(attend(qk_layout(qc), kc_t, vc_t), (0, 2, 1, 3))
    return y, _diff_out(oc, norm_w, w_out, lam_init).astype(hc.dtype)


def swiglu(t, w_gu, w_down):
    g, u = jnp.split(t @ w_gu, 2, axis=-1)
    return (jax.nn.silu(g) * u) @ w_down


def moe_swiglu(t, router, w_gu, w_down):
    shp = t.shape
    xt = t.reshape(-1, shp[-1])
    logits = (xt @ router).astype(jnp.float32)
    top_v, top_i = lax.top_k(logits, TOP_K)
    top_w = jax.nn.softmax(top_v, axis=-1)
    gates = jnp.sum(jax.nn.one_hot(top_i, N_EXPERTS, dtype=jnp.float32) * top_w[..., None], axis=1)
    y = jnp.zeros_like(xt)
    for e in range(N_EXPERTS):
        y = y + gates[:, e:e + 1].astype(xt.dtype) * swiglu(xt, w_gu[e], w_down[e])
    return y.reshape(shp)


def setup_inputs(seed: int = 0) -> dict:
    key = jax.random.key(seed)
    ks = iter(jax.random.split(key, 32))
    n_a = (DEPTH + 1) // 2
    n_b = DEPTH // 2
    beta = (8.0 * DEPTH) ** -0.25
    D = D_MODEL

    def nrm(shape, scale):
        return jax.random.normal(next(ks), shape, jnp.float32) * scale

    x = nrm((BATCH, SEQ, D), 1.0)
    c = nrm((BATCH, D), 1.0)
    ctx = nrm((BATCH, CTX_LEN, D), 1.0)
    c_ctx = nrm((D,), 1.0)
    ada_w = nrm((DEPTH, D, 6 * D), D ** -0.5)
    ada_b = nrm((DEPTH, 6 * D), 0.02)
    ln_w = 1.0 + nrm((DEPTH, 2, D), 0.02)
    ln_b = nrm((DEPTH, 2, D), 0.02)
    gdn_w_in = nrm((n_a, D, GDN_IN), D ** -0.5)
    gdn_conv_w = nrm((n_a, GDN_CONV, GDN_QKV), GDN_CONV ** -0.5)
    gdn_a_log = jnp.log(jax.random.uniform(next(ks), (n_a, 2, GDN_V_HEADS), jnp.float32, 1.0, 16.0))
    dt = jnp.exp(jax.random.uniform(next(ks), (n_a, 2, GDN_V_HEADS), jnp.float32,
                                    math.log(1e-3), math.log(1e-1)))
    gdn_dt_bias = dt + jnp.log(-jnp.expm1(-dt))
    gdn_norm_w = 1.0 + nrm((n_a, GDN_DV), 0.02)
    gdn_w_out = nrm((n_a, GDN_V, D), GDN_V ** -0.5 * beta)
    ffn_w_gu = nrm((n_a, D, 2 * D_FF), D ** -0.5)
    ffn_w_down = nrm((n_a, D_FF, D), D_FF ** -0.5 * beta)
    diff_w_in = nrm((n_b, D, DIFF_IN), D ** -0.5)
    diff_lambda = nrm((n_b, 4, DIFF_DH), 0.1)
    diff_norm_w = 1.0 + nrm((n_b, DIFF_DV), 0.02)
    diff_w_out = nrm((n_b, DIFF_V, D), DIFF_V ** -0.5 * beta)
    moe_router = nrm((n_b, D, N_EXPERTS), D ** -0.5)
    moe_w_gu = nrm((n_b, N_EXPERTS, D, 2 * D_FF_EXPERT), D ** -0.5)
    moe_w_down = nrm((n_b, N_EXPERTS, D_FF_EXPERT, D), D_FF_EXPERT ** -0.5 * beta)
    return {'x': x, 'c': c, 'ctx': ctx, 'c_ctx': c_ctx,
            'ada_w': ada_w, 'ada_b': ada_b, 'ln_w': ln_w, 'ln_b': ln_b,
            'gdn_w_in': gdn_w_in, 'gdn_conv_w': gdn_conv_w, 'gdn_a_log': gdn_a_log,
            'gdn_dt_bias': gdn_dt_bias, 'gdn_norm_w': gdn_norm_w, 'gdn_w_out': gdn_w_out,
            'ffn_w_gu': ffn_w_gu, 'ffn_w_down': ffn_w_down,
            'diff_w_in': diff_w_in, 'diff_lambda': diff_lambda, 'diff_norm_w': diff_norm_w,
            'diff_w_out': diff_w_out,
            'moe_router': moe_router, 'moe_w_gu': moe_w_gu, 'moe_w_down': moe_w_down}


def reference(x, c, ctx, c_ctx, ada_w, ada_b, ln_w, ln_b,
              gdn_w_in, gdn_conv_w, gdn_a_log, gdn_dt_bias, gdn_norm_w, gdn_w_out,
              ffn_w_gu, ffn_w_down,
              diff_w_in, diff_lambda, diff_norm_w, diff_w_out,
              moe_router, moe_w_gu, moe_w_down):
    alpha = (2.0 * DEPTH) ** 0.25
    n_ctx = ctx.shape[1]

    def channel_mixer(t, i, j):
        if i % 2 == 0:
            return swiglu(t, ffn_w_gu[j], ffn_w_down[j])
        return moe_swiglu(t, moe_router[j], moe_w_gu[j], moe_w_down[j])

    for i in range(DEPTH):
        last = i == DEPTH - 1
        j = i // 2
        mod = [m[:, None, :] for m in modulation(c, ada_w[i], ada_b[i])]
        mod_c = modulation(c_ctx, ada_w[i], ada_b[i])
        h = x * (1.0 + mod[1]) + mod[0]
        hc = ctx * (1.0 + mod_c[1]) + mod_c[0]
        if i % 2 == 0:
            y, yc = gdn_mixer(h, hc, gdn_w_in[j], gdn_conv_w[j], gdn_a_log[j], gdn_dt_bias[j],
                              gdn_norm_w[j], gdn_w_out[j], last)
        else:
            y, yc = diff_mixer(h, hc, diff_w_in[j], diff_lambda[j], diff_norm_w[j], diff_w_out[j], i, last)
        x = layer_norm(alpha * x + mod[2] * y, ln_w[i, 0], ln_b[i, 0])
        if last:
            f = channel_mixer(x * (1.0 + mod[4]) + mod[3], i, j)
            x = layer_norm(alpha * x + mod[5] * f, ln_w[i, 1], ln_b[i, 1])
        else:
            ctx = layer_norm(alpha * ctx + mod_c[2] * yc, ln_w[i, 0], ln_b[i, 0])
            h_all = jnp.concatenate([ctx * (1.0 + mod_c[4]) + mod_c[3],
                                     x * (1.0 + mod[4]) + mod[3]], axis=1)
            f = channel_mixer(h_all, i, j)
            ctx = layer_norm(alpha * ctx + mod_c[5] * f[:, :n_ctx], ln_w[i, 1], ln_b[i, 1])
            x = layer_norm(alpha * x + mod[5] * f[:, n_ctx:], ln_w[i, 1], ln_b[i, 1])
    return x
```

```python
import functools
import math

import jax
import jax.numpy as jnp
from jax import lax
from jax.experimental import pallas as pl
from jax.experimental.pallas import tpu as pltpu

F32 = jnp.float32
BF16 = jnp.bfloat16

GDN_QK_HEADS = 16
GDN_V_HEADS = 32
GDN_DK = 128
GDN_DV = 128
GDN_CONV = 5
GDN_CHUNK = 64
GDN_Q = GDN_QK_HEADS * GDN_DK
GDN_V = GDN_V_HEADS * GDN_DV
GDN_QKV = 2 * GDN_Q + GDN_V
DIFF_HEADS = 8
DIFF_DH = 128
DIFF_DV = 2 * DIFF_DH
DIFF_Q = DIFF_HEADS * 2 * DIFF_DH
GRID_W = 64
ROPE_BASE = 10000.0
ROPE_AXIS_DIM = DIFF_DH // 2
N_EXPERTS = 8
LN_EPS = 1e-5
RMS_EPS = 1e-6

V7X_VMEM_BYTES = 64 * 1024 * 1024
VMEM_LIMIT = V7X_VMEM_BYTES - 8 * 1024 * 1024
LANES = 128
ROW_TILE = 256
CHUNKS_PER_TILE = ROW_TILE // GDN_CHUNK
MOE_TILE = 256


def _params(*sem):
    return pltpu.CompilerParams(dimension_semantics=sem, vmem_limit_bytes=VMEM_LIMIT)


def _row_tile(m, preferred):
    t = preferred
    while m % t and t > ROW_TILE:
        t //= 2
    assert m % t == 0
    return t


def _dot(a, b):
    return jnp.dot(a, b, preferred_element_type=F32)


def _dot_nt(a, b):
    return lax.dot_general(a, b, (((1,), (1,)), ((), ())), preferred_element_type=F32)


def _dot_tn(a, b):
    return lax.dot_general(a, b, (((0,), (0,)), ((), ())), preferred_element_type=F32)


def _silu(x):
    return x * jax.nn.sigmoid(x)


def _ada_kernel(c_ref, w_ref, b_ref, o_ref):
    cond = _silu(c_ref[...])
    o_ref[0] = jnp.dot(cond, w_ref[0], preferred_element_type=F32,
                       precision=lax.Precision.HIGHEST) + b_ref[0]


def _ada_modulation(cond, ada_w, ada_b):
    depth, d, n = ada_w.shape
    rows = cond.shape[0]
    tn = 1024
    return pl.pallas_call(
        _ada_kernel,
        out_shape=jax.ShapeDtypeStruct((depth, rows, n), F32),
        grid=(depth, n // tn),
        in_specs=[pl.BlockSpec((rows, d), lambda l, j: (0, 0)),
                  pl.BlockSpec((1, d, tn), lambda l, j: (l, 0, j)),
                  pl.BlockSpec((1, 1, tn), lambda l, j: (l, 0, j))],
        out_specs=pl.BlockSpec((1, rows, tn), lambda l, j: (l, 0, j)),
        compiler_params=_params("parallel", "parallel"),
        name="ada_modulation",
    )(cond, ada_w, ada_b.reshape(depth, 1, n))


def _mod_index(r, tiles_per_batch, ctx_tiles, ctx_row):
    return jnp.where(r % tiles_per_batch < ctx_tiles, ctx_row, r // tiles_per_batch)


def _modulate_kernel(x_ref, m_ref, o_ref):
    m = m_ref[0]
    o_ref[...] = (x_ref[...] * (1.0 + m[1:2]) + m[0:1]).astype(o_ref.dtype)


def _modulate(x, mod, tiles_per_batch, ctx_tiles):
    rows, d = x.shape
    ctx_row = mod.shape[0] - 1
    midx = functools.partial(_mod_index, tiles_per_batch=tiles_per_batch,
                             ctx_tiles=ctx_tiles, ctx_row=ctx_row)
    return pl.pallas_call(
        _modulate_kernel,
        out_shape=jax.ShapeDtypeStruct((rows, d), BF16),
        grid=(rows // ROW_TILE,),
        in_specs=[pl.BlockSpec((ROW_TILE, d), lambda r: (r, 0)),
                  pl.BlockSpec((1, 6, d), lambda r: (midx(r), 0, 0))],
        out_specs=pl.BlockSpec((ROW_TILE, d), lambda r: (r, 0)),
        compiler_params=_params("parallel"),
        name="modulate",
    )(x, mod)


def _layer_norm_rows(t, w, b):
    mu = jnp.mean(t, axis=-1, keepdims=True)
    tc = t - mu
    var = jnp.mean(tc * tc, axis=-1, keepdims=True)
    return tc * lax.rsqrt(var + LN_EPS) * w + b


def _res_ln_kernel(x_ref, y_ref, mg_ref, mn_ref, lnw_ref, lnb_ref, xo_ref, ho_ref,
                   *, alpha, gate_idx, shift_idx):
    gate = mg_ref[0][gate_idx:gate_idx + 1]
    t = alpha * x_ref[...] + gate * y_ref[...].astype(F32)
    xn = _layer_norm_rows(t, lnw_ref[...], lnb_ref[...])
    xo_ref[...] = xn
    mn = mn_ref[0]
    h = xn * (1.0 + mn[shift_idx + 1:shift_idx + 2]) + mn[shift_idx:shift_idx + 1]
    ho_ref[...] = h.astype(ho_ref.dtype)


def _res_ln(x, y, mod_gate, gate_idx, mod_next, shift_idx, ln_w, ln_b, alpha,
            tiles_per_batch, ctx_tiles, h_dtype):
    rows, d = x.shape
    ctx_row = mod_gate.shape[0] - 1
    midx = functools.partial(_mod_index, tiles_per_batch=tiles_per_batch,
                             ctx_tiles=ctx_tiles, ctx_row=ctx_row)
    row_spec = pl.BlockSpec((ROW_TILE, d), lambda r: (r, 0))
    mod_spec = pl.BlockSpec((1, 6, d), lambda r: (midx(r), 0, 0))
    vec_spec = pl.BlockSpec((1, d), lambda r: (0, 0))
    return pl.pallas_call(
        functools.partial(_res_ln_kernel, alpha=alpha, gate_idx=gate_idx, shift_idx=shift_idx),
        out_shape=(jax.ShapeDtypeStruct((rows, d), F32), jax.ShapeDtypeStruct((rows, d), h_dtype)),
        grid=(rows // ROW_TILE,),
        in_specs=[row_spec, row_spec, mod_spec, mod_spec, vec_spec, vec_spec],
        out_specs=(row_spec, row_spec),
        compiler_params=_params("parallel"),
        name="residual_layer_norm",
    )(x, y, mod_gate, mod_next, ln_w.reshape(1, d), ln_b.reshape(1, d))


def _mm_kernel(a_ref, w_ref, o_ref, wbf_ref):
    @pl.when(pl.program_id(1) == 0)
    def _():
        wbf_ref[...] = w_ref[...].astype(BF16)

    o_ref[...] = _dot(a_ref[...], wbf_ref[...]).astype(o_ref.dtype)


def _matmul(a, w, *, tm, tn, out_dtype, col_start=0, n_cols=None):
    m, k = a.shape
    tm = _row_tile(m, tm)
    n_cols = w.shape[1] - col_start if n_cols is None else n_cols
    assert m % tm == 0 and n_cols % tn == 0 and col_start % tn == 0
    off = col_start // tn
    return pl.pallas_call(
        _mm_kernel,
        out_shape=jax.ShapeDtypeStruct((m, n_cols), out_dtype),
        grid=(n_cols // tn, m // tm),
        in_specs=[pl.BlockSpec((tm, k), lambda j, i: (i, 0)),
                  pl.BlockSpec((k, tn), lambda j, i: (0, j + off))],
        out_specs=pl.BlockSpec((tm, tn), lambda j, i: (i, j)),
        scratch_shapes=[pltpu.VMEM((k, tn), BF16)],
        compiler_params=_params("parallel", "arbitrary"),
        name="matmul",
    )(a, w)


def _swiglu_up_kernel(a_ref, wg_ref, wu_ref, o_ref, wgb_ref, wub_ref):
    @pl.when(pl.program_id(1) == 0)
    def _():
        wgb_ref[...] = wg_ref[...].astype(BF16)
        wub_ref[...] = wu_ref[...].astype(BF16)

    a = a_ref[...]
    g = _dot(a, wgb_ref[...])
    u = _dot(a, wub_ref[...])
    o_ref[...] = (_silu(g) * u).astype(o_ref.dtype)


def _swiglu_up(a, w_gu, *, tm, tn):
    m, k = a.shape
    tm = _row_tile(m, tm)
    ff = w_gu.shape[1] // 2
    assert m % tm == 0 and ff % tn == 0
    nj = ff // tn
    return pl.pallas_call(
        _swiglu_up_kernel,
        out_shape=jax.ShapeDtypeStruct((m, ff), BF16),
        grid=(nj, m // tm),
        in_specs=[pl.BlockSpec((tm, k), lambda j, i: (i, 0)),
                  pl.BlockSpec((k, tn), lambda j, i: (0, j)),
                  pl.BlockSpec((k, tn), lambda j, i: (0, j + nj))],
        out_specs=pl.BlockSpec((tm, tn), lambda j, i: (i, j)),
        scratch_shapes=[pltpu.VMEM((k, tn), BF16), pltpu.VMEM((k, tn), BF16)],
        compiler_params=_params("parallel", "arbitrary"),
        name="swiglu_up",
    )(a, w_gu, w_gu)


def _gdn_conv_kernel(cur_ref, prev_ref, next_ref, w_ref, o_ref, xe_ref, *, n_qk_blocks, q_scale):
    i = pl.program_id(1)
    j = pl.program_id(2)
    n_tiles = pl.num_programs(1)
    prev_ok = (i >= 2).astype(F32)
    next_ok = jnp.logical_and(i >= 1, i < n_tiles - 1).astype(F32)
    xe_ref[0:8, :] = prev_ref[0].astype(F32)[8:16] * prev_ok
    xe_ref[8:8 + ROW_TILE, :] = cur_ref[0].astype(F32)
    xe_ref[8 + ROW_TILE:16 + ROW_TILE, :] = next_ref[0].astype(F32)[0:8] * next_ok
    w = w_ref[...]
    pad = (GDN_CONV - 1) // 2
    acc = None
    for t in range(GDN_CONV):
        term = xe_ref[pl.ds(8 - pad + t, ROW_TILE), :] * w[t:t + 1]
        acc = term if acc is None else acc + term
    y = _silu(acc)
    cb = y.shape[1]

    @pl.when(j < n_qk_blocks)
    def _():
        scale = jnp.where(j < n_qk_blocks // 2, q_scale, 1.0)
        for g in range(cb // GDN_DK):
            yg = y[:, g * GDN_DK:(g + 1) * GDN_DK]
            inv = lax.rsqrt(jnp.sum(yg * yg, axis=-1, keepdims=True) + 1e-6)
            o_ref[0, :, g * GDN_DK:(g + 1) * GDN_DK] = (yg * (inv * scale)).astype(o_ref.dtype)

    @pl.when(j >= n_qk_blocks)
    def _():
        o_ref[0] = y.astype(o_ref.dtype)


def _gdn_conv(pqz, conv_w, batch):
    rows, _ = pqz.shape
    t_all = rows // batch
    x = pqz.reshape(batch, t_all, pqz.shape[1])
    cb = 256
    n_tiles = t_all // ROW_TILE
    halo = 16
    last_halo = t_all // halo - 1
    per_tile = ROW_TILE // halo
    return pl.pallas_call(
        functools.partial(_gdn_conv_kernel, n_qk_blocks=2 * GDN_Q // cb, q_scale=GDN_DK ** -0.5),
        out_shape=jax.ShapeDtypeStruct((batch, t_all, GDN_QKV), BF16),
        grid=(batch, n_tiles, GDN_QKV // cb),
        in_specs=[pl.BlockSpec((1, ROW_TILE, cb), lambda b, i, j: (b, i, j)),
                  pl.BlockSpec((1, halo, cb), lambda b, i, j: (b, jnp.maximum(i * per_tile - 1, 0), j)),
                  pl.BlockSpec((1, halo, cb),
                               lambda b, i, j: (b, jnp.minimum((i + 1) * per_tile, last_halo), j)),
                  pl.BlockSpec((GDN_CONV, cb), lambda b, i, j: (0, j))],
        out_specs=pl.BlockSpec((1, ROW_TILE, cb), lambda b, i, j: (b, i, j)),
        scratch_shapes=[pltpu.VMEM((ROW_TILE + 16, cb), F32)],
        compiler_params=_params("parallel", "parallel", "parallel"),
        name="gdn_conv",
    )(x, x, x, conv_w)


def _gdn_gate_kernel(ab_ref, alog_ref, dtb_ref, o_ref):
    ab = ab_ref[...]
    half = LANES // 2
    quarter = LANES // 4
    lane = lax.broadcasted_iota(jnp.int32, ab.shape, 1)
    beta = jax.nn.sigmoid(ab)
    x = ab + dtb_ref[...]
    softplus = jnp.maximum(x, 0.0) + jnp.log1p(jnp.exp(-jnp.abs(x)))
    g = jnp.where(lane >= half, -jnp.exp(alog_ref[...]) * softplus, 0.0)
    g = pltpu.roll(g, half, axis=1)
    n = ab.shape[0]
    ri = lax.broadcasted_iota(jnp.int32, (n, n), 0)
    ci = lax.broadcasted_iota(jnp.int32, (n, n), 1)
    same = (ri // GDN_CHUNK) == (ci // GDN_CHUNK)
    hi = lax.Precision.HIGHEST
    m_fwd = jnp.logical_and(same, ci <= ri).astype(F32)
    m_bwd = jnp.logical_and(same, ci >= ri).astype(F32)
    gc_f = jnp.dot(m_fwd, g, preferred_element_type=F32, precision=hi)
    gc_b = jnp.dot(m_bwd, g, preferred_element_type=F32, precision=hi)
    tot = jnp.dot(same.astype(F32), g, preferred_element_type=F32, precision=hi)
    gc = jnp.where(lane < quarter, gc_f, gc_b)
    eg = jnp.exp(gc)
    ek = jnp.exp(tot - gc)
    for d in range(2):
        parts = [pltpu.roll(val, (q * quarter - d * quarter) % LANES, axis=1)
                 for q, val in enumerate((beta, gc, eg, ek))]
        o_ref[d] = jnp.where(lane < quarter, parts[0],
                             jnp.where(lane < 2 * quarter, parts[1],
                                       jnp.where(lane < 3 * quarter, parts[2], parts[3])))


def _gdn_gates(ab, a_log, dt_bias):
    rows = ab.shape[0]
    zeros = jnp.zeros((LANES // 2,), F32)
    alog = jnp.concatenate([zeros, a_log.reshape(-1).astype(F32)]).reshape(1, LANES)
    dtb = jnp.concatenate([zeros, dt_bias.reshape(-1).astype(F32)]).reshape(1, LANES)
    return pl.pallas_call(
        _gdn_gate_kernel,
        out_shape=jax.ShapeDtypeStruct((2, rows, LANES), F32),
        grid=(rows // ROW_TILE,),
        in_specs=[pl.BlockSpec((ROW_TILE, LANES), lambda r: (r, 0)),
                  pl.BlockSpec((1, LANES), lambda r: (0, 0)),
                  pl.BlockSpec((1, LANES), lambda r: (0, 0))],
        out_specs=pl.BlockSpec((2, ROW_TILE, LANES), lambda r: (0, r, 0)),
        compiler_params=_params("parallel"),
        name="gdn_gates",
    )(ab, alog, dtb)


def _block_masks(n):
    ri = lax.broadcasted_iota(jnp.int32, (n, n), 0)
    ci = lax.broadcasted_iota(jnp.int32, (n, n), 1)
    masks = [(ri >> 1) == (ci >> 1)]
    for level in range(1, int(math.log2(n))):
        masks.append(jnp.logical_and((ri >> (level + 1)) == (ci >> (level + 1)),
                                     (ri >> level) != (ci >> level)))
    return masks


def _bmm(a, b):
    return jnp.einsum('hij,hjk->hik', a, b, preferred_element_type=F32)


def _bmm_nt(a, b):
    return jnp.einsum('hik,hjk->hij', a, b, preferred_element_type=F32)


def _bmm_tn(a, b):
    return jnp.einsum('hki,hkj->hij', a, b, preferred_element_type=F32)


def _unit_triangular_inverse(a, eye, masks):
    d = eye - jnp.where(masks[0], a, 0.0)
    for mask in masks[1:]:
        db = d.astype(BF16)
        ad = _bmm(jnp.where(mask, a, 0.0).astype(BF16), db)
        d = d - _bmm(db, ad.astype(BF16))
    return d


def _gdn_scan_kernel(q_ref, k_ref, v_ref, g_ref, gr_ref, o_ref, s_ref):
    d = pl.program_id(1)
    s = pl.program_id(2)

    @pl.when(s == 0)
    def _():
        s_ref[...] = jnp.zeros_like(s_ref)

    c = GDN_CHUNK
    n_h = GDN_V_HEADS
    rep = GDN_V_HEADS // GDN_QK_HEADS
    rev = d == 1
    ri = lax.broadcasted_iota(jnp.int32, (c, c), 0)
    ci = lax.broadcasted_iota(jnp.int32, (c, c), 1)
    later = jnp.where(rev, ci, ri)
    earlier = jnp.where(rev, ri, ci)
    incl = later >= earlier
    strict = later > earlier
    eye = (ri == ci).astype(F32)
    masks = _block_masks(c)

    def chunk_step(step, carry):
        cc = jnp.where(rev, CHUNKS_PER_TILE - 1 - step, step)
        off = pl.multiple_of(cc * c, c)
        q_all = q_ref[0, pl.ds(off, c), :]
        k_all = k_ref[0, pl.ds(off, c), :]
        v_all = v_ref[0, pl.ds(off, c), :]
        gates = g_ref[0, pl.ds(off, c), :]
        gc_rows = gr_ref[0, 0, cc]
        qs = [q_all[:, i * GDN_DK:(i + 1) * GDN_DK] for i in range(GDN_QK_HEADS)]
        ks = [k_all[:, i * GDN_DK:(i + 1) * GDN_DK] for i in range(GDN_QK_HEADS)]
        k_qk = jnp.stack(ks)
        kk_raw = _bmm_nt(k_qk, k_qk)
        qk_raw = _bmm_nt(jnp.stack(qs), k_qk)

        def col(quantity, h):
            lane = quantity * n_h + h
            return gates[:, lane:lane + 1]

        a_l, qk_l, rhs_l, qg_l, kt_l, egl_l = [], [], [], [], [], []
        for h in range(n_h):
            beta, gc, eg, ek = (col(i, h) for i in range(4))
            gdiff = gc - gc_rows[h:h + 1, :]
            decay = jnp.where(incl, jnp.exp(jnp.where(incl, gdiff, 0.0)), 0.0)
            a_l.append(jnp.where(strict, kk_raw[h // rep] * (beta * decay), 0.0))
            qk_l.append((qk_raw[h // rep] * decay).astype(BF16))
            kf = ks[h // rep].astype(F32)
            vf = v_all[:, h * GDN_DV:(h + 1) * GDN_DV].astype(F32)
            rhs_l.append(jnp.concatenate([vf * beta, kf * (beta * eg)], axis=1).astype(BF16))
            eg_b = jnp.broadcast_to(eg, (c, GDN_DK))
            qg_l.append((qs[h // rep].astype(F32) * eg_b).astype(BF16))
            kt_l.append((kf * ek).astype(BF16))
            egl_l.append(jnp.where(rev, eg_b[0:1, :], eg_b[c - 1:c, :]))
        t_inv = _unit_triangular_inverse(jnp.stack(a_l), eye, masks)
        uw = _bmm(t_inv.astype(BF16), jnp.stack(rhs_l))
        u = uw[:, :, 0:GDN_DV]
        w = uw[:, :, GDN_DV:]
        state = s_ref[...]
        state_b = state.astype(BF16)
        v_new = u - _bmm(w.astype(BF16), state_b)
        v_new_b = v_new.astype(BF16)
        o = _bmm(jnp.stack(qg_l), state_b) + _bmm(jnp.stack(qk_l), v_new_b)
        s_ref[...] = state * jnp.stack(egl_l) + _bmm_tn(jnp.stack(kt_l), v_new_b)
        for h in range(n_h):
            o_ref[0, 0, pl.ds(off, c), h * GDN_DV:(h + 1) * GDN_DV] = o[h].astype(o_ref.dtype)
        return carry

    lax.fori_loop(0, CHUNKS_PER_TILE, chunk_step, 0)


def _gdn_scan(qkv, gates, gc_rows, batch):
    _, t_all, _ = qkv.shape
    n_tiles = t_all // ROW_TILE

    def tile(d, s):
        return jnp.where(d == 0, s, jnp.where(s == 0, 0, n_tiles - s))

    return pl.pallas_call(
        _gdn_scan_kernel,
        out_shape=jax.ShapeDtypeStruct((2, batch, t_all, GDN_V), BF16),
        grid=(batch, 2, n_tiles),
        in_specs=[
            pl.BlockSpec((1, ROW_TILE, GDN_Q), lambda b, d, s: (b, tile(d, s), 0)),
            pl.BlockSpec((1, ROW_TILE, GDN_Q), lambda b, d, s: (b, tile(d, s), 1)),
            pl.BlockSpec((1, ROW_TILE, GDN_V), lambda b, d, s: (b, tile(d, s), 1)),
            pl.BlockSpec((1, ROW_TILE, LANES), lambda b, d, s: (d, b * n_tiles + tile(d, s), 0)),
            pl.BlockSpec((1, 1, CHUNKS_PER_TILE, GDN_V_HEADS, GDN_CHUNK),
                         lambda b, d, s: (b, d, tile(d, s), 0, 0)),
        ],
        out_specs=pl.BlockSpec((1, 1, ROW_TILE, GDN_V), lambda b, d, s: (d, b, tile(d, s), 0)),
        scratch_shapes=[pltpu.VMEM((GDN_V_HEADS, GDN_DK, GDN_DV), F32)],
        compiler_params=_params("parallel", "arbitrary", "arbitrary"),
        name="gdn_scan",
    )(qkv, qkv, qkv, gates, gc_rows)


def _gdn_norm_gate_kernel(of_ref, ob_ref, z_ref, w_ref, o_ref):
    o = of_ref[0].astype(F32) + ob_ref[0].astype(F32)
    z = z_ref[...].astype(F32)
    w = w_ref[...]
    for g in range(o.shape[1] // GDN_DV):
        sl = slice(g * GDN_DV, (g + 1) * GDN_DV)
        og = o[:, sl]
        inv = lax.rsqrt(jnp.mean(og * og, axis=-1, keepdims=True) + RMS_EPS)
        o_ref[:, sl] = (og * inv * w * _silu(z[:, sl])).astype(o_ref.dtype)


def _gdn_norm_gate(o2, pqz, norm_w):
    _, rows, width = o2.shape
    cb = 512
    z_off = GDN_QKV // cb
    return pl.pallas_call(
        _gdn_norm_gate_kernel,
        out_shape=jax.ShapeDtypeStruct((rows, width), BF16),
        grid=(rows // ROW_TILE, width // cb),
        in_specs=[pl.BlockSpec((1, ROW_TILE, cb), lambda r, j: (0, r, j)),
                  pl.BlockSpec((1, ROW_TILE, cb), lambda r, j: (1, r, j)),
                  pl.BlockSpec((ROW_TILE, cb), lambda r, j: (r, z_off + j)),
                  pl.BlockSpec((1, GDN_DV), lambda r, j: (0, 0))],
        out_specs=pl.BlockSpec((ROW_TILE, cb), lambda r, j: (r, j)),
        compiler_params=_params("parallel", "parallel"),
        name="gdn_norm_gate",
    )(o2, o2, pqz, norm_w.reshape(1, GDN_DV))


def _gdn_mixer(h, batch, w_in, conv_w, a_log, dt_bias, norm_w, w_out):
    rows = h.shape[0]
    t_all = rows // batch
    pqz = _matmul(h, w_in, tm=1024, tn=1024, out_dtype=BF16, col_start=0, n_cols=GDN_QKV + GDN_V)
    ab = _matmul(h, w_in, tm=1024, tn=LANES, out_dtype=F32, col_start=GDN_QKV + GDN_V, n_cols=LANES)
    gates = _gdn_gates(ab, a_log, dt_bias)
    n_chunks = t_all // GDN_CHUNK
    gc_rows = gates[:, :, GDN_V_HEADS:2 * GDN_V_HEADS]
    gc_rows = gc_rows.reshape(2, batch, n_chunks, GDN_CHUNK, GDN_V_HEADS).transpose(1, 0, 2, 4, 3)
    qkv = _gdn_conv(pqz, conv_w, batch)
    o2 = _gdn_scan(qkv, gates, gc_rows, batch)
    og = _gdn_norm_gate(o2.reshape(2, rows, GDN_V), pqz, norm_w)
    return _matmul(og, w_out, tm=1024, tn=512, out_dtype=F32)


def _rope_tables(n_ctx, n_lat):
    rows = n_lat // GRID_W
    row = jnp.repeat(jnp.arange(rows, dtype=F32), GRID_W)
    col = jnp.tile(jnp.arange(GRID_W, dtype=F32), rows)
    inv_freq = 1.0 / (ROPE_BASE ** (jnp.arange(0, ROPE_AXIS_DIM, 2, dtype=F32) / ROPE_AXIS_DIM))
    ang_r = row[:, None] * inv_freq
    ang_c = col[:, None] * inv_freq
    cos = jnp.concatenate([jnp.cos(ang_r), jnp.cos(ang_r), jnp.cos(ang_c), jnp.cos(ang_c)], axis=1)
    sin = jnp.concatenate([-jnp.sin(ang_r), jnp.sin(ang_r), -jnp.sin(ang_c), jnp.sin(ang_c)], axis=1)
    cos = jnp.concatenate([jnp.ones((n_ctx, DIFF_DH), F32), cos], axis=0)
    sin = jnp.concatenate([jnp.zeros((n_ctx, DIFF_DH), F32), sin], axis=0)
    return cos, sin


def _rope_kernel(x_ref, cos_ref, sin_ref, o_ref, *, n_q_blocks, q_scale):
    j = pl.program_id(2)
    scale = jnp.where(j < n_q_blocks, q_scale, 1.0)
    cos = cos_ref[...] * scale
    sin = sin_ref[...] * scale
    quarter = ROPE_AXIS_DIM // 2
    lane = lax.broadcasted_iota(jnp.int32, cos.shape, 1)
    first = (lane % ROPE_AXIS_DIM) < quarter
    x = x_ref[0].astype(F32)
    for g in range(x.shape[1] // DIFF_DH):
        sl = slice(g * DIFF_DH, (g + 1) * DIFF_DH)
        xg = x[:, sl]
        partner = jnp.where(first, pltpu.roll(xg, DIFF_DH - quarter, axis=1),
                            pltpu.roll(xg, quarter, axis=1))
        o_ref[0, :, sl] = (xg * cos + partner * sin).astype(o_ref.dtype)


def _rope(qkv, batch, cos, sin):
    rows, _ = qkv.shape
    t_all = rows // batch
    x = qkv.reshape(batch, t_all, qkv.shape[1])
    cb = 2 * DIFF_DH
    return pl.pallas_call(
        functools.partial(_rope_kernel, n_q_blocks=DIFF_Q // cb, q_scale=DIFF_DH ** -0.5),
        out_shape=jax.ShapeDtypeStruct((batch, t_all, 2 * DIFF_Q), BF16),
        grid=(batch, t_all // ROW_TILE, 2 * DIFF_Q // cb),
        in_specs=[pl.BlockSpec((1, ROW_TILE, cb), lambda b, i, j: (b, i, j)),
                  pl.BlockSpec((ROW_TILE, DIFF_DH), lambda b, i, j: (i, 0)),
                  pl.BlockSpec((ROW_TILE, DIFF_DH), lambda b, i, j: (i, 0))],
        out_specs=pl.BlockSpec((1, ROW_TILE, cb), lambda b, i, j: (b, i, j)),
        compiler_params=_params("parallel", "parallel", "parallel"),
        name="rope",
    )(x, cos, sin)


def _diff_attn_kernel(lam_ref, nw_ref, q_ref, k_ref, v_ref, o_ref, *, lam_init):
    lp = lam_ref[...]
    lam = (jnp.exp(jnp.sum(lp[0:1] * lp[1:2], axis=-1, keepdims=True))
           - jnp.exp(jnp.sum(lp[2:3] * lp[3:4], axis=-1, keepdims=True)) + lam_init)
    q = q_ref[0]
    k = k_ref[0]

    def softmax_parts(m):
        sl = slice(m * DIFF_DH, (m + 1) * DIFF_DH)
        s = _dot_nt(q[:, sl], k[:, sl])
        p = jnp.exp(s - jnp.max(s, axis=-1, keepdims=True))
        return p, jnp.sum(p, axis=-1, keepdims=True)

    p1, l1 = softmax_parts(0)
    p2, l2 = softmax_parts(1)
    a = p1 * (1.0 / l1) - p2 * (lam / l2)
    o = _dot(a.astype(BF16), v_ref[0])
    inv = lax.rsqrt(jnp.mean(o * o, axis=-1, keepdims=True) + RMS_EPS)
    o_ref[0] = (o * inv * nw_ref[...] * (1.0 - lam_init)).astype(o_ref.dtype)


def _diff_attention(qk, qkv, batch, n_ctx, lam_p, norm_w, lam_init):
    _, t_all, _ = qk.shape
    n_lat = t_all - n_ctx
    tq = ROW_TILE
    assert n_ctx % tq == 0
    ctx_tiles = n_ctx // tq
    v3 = qkv.reshape(batch, t_all, qkv.shape[1])
    k_off = DIFF_Q // DIFF_DV
    v_off = 2 * DIFF_Q // DIFF_DV
    return pl.pallas_call(
        functools.partial(_diff_attn_kernel, lam_init=lam_init),
        out_shape=jax.ShapeDtypeStruct((batch, n_lat, DIFF_HEADS * DIFF_DV), BF16),
        grid=(batch, DIFF_HEADS, n_lat // tq),
        in_specs=[pl.BlockSpec((4, DIFF_DH), lambda b, h, i: (0, 0)),
                  pl.BlockSpec((1, DIFF_DV), lambda b, h, i: (0, 0)),
                  pl.BlockSpec((1, tq, DIFF_DV), lambda b, h, i: (b, i + ctx_tiles, h)),
                  pl.BlockSpec((1, t_all, DIFF_DV), lambda b, h, i: (b, 0, k_off + h)),
                  pl.BlockSpec((1, t_all, DIFF_DV), lambda b, h, i: (b, 0, v_off + h))],
        out_specs=pl.BlockSpec((1, tq, DIFF_DV), lambda b, h, i: (b, i, h)),
        compiler_params=_params("parallel", "parallel", "arbitrary"),
        name="diff_attention",
    )(lam_p.astype(F32), norm_w.reshape(1, DIFF_DV), qk, qk, v3)


def _router_kernel(h_ref, r_ref, idx_ref, w_ref):
    logits = _dot_nt(r_ref[...].astype(BF16), h_ref[...].astype(BF16))
    e = lax.broadcasted_iota(jnp.int32, logits.shape, 0)
    n_e = logits.shape[0]
    m1 = jnp.max(logits, axis=0, keepdims=True)
    i1 = jnp.min(jnp.where(logits == m1, e, n_e), axis=0, keepdims=True)
    rest = jnp.where(e == i1, -jnp.inf, logits)
    m2 = jnp.max(rest, axis=0, keepdims=True)
    i2 = jnp.min(jnp.where(rest == m2, e, n_e), axis=0, keepdims=True)
    t = jnp.exp(m2 - m1)
    idx_ref[0:1, :] = i1
    idx_ref[1:2, :] = i2
    w_ref[0:1, :] = 1.0 / (1.0 + t)
    w_ref[1:2, :] = t / (1.0 + t)


def _router(h, router):
    rows, d = h.shape
    tm = _row_tile(rows, 1024)
    return pl.pallas_call(
        _router_kernel,
        out_shape=(jax.ShapeDtypeStruct((2, rows), jnp.int32), jax.ShapeDtypeStruct((2, rows), F32)),
        grid=(rows // tm,),
        in_specs=[pl.BlockSpec((tm, d), lambda i: (i, 0)),
                  pl.BlockSpec((N_EXPERTS, d), lambda i: (0, 0))],
        out_specs=(pl.BlockSpec((2, tm), lambda i: (0, i)), pl.BlockSpec((2, tm), lambda i: (0, i))),
        compiler_params=_params("parallel"),
        name="moe_router",
    )(h, router.T)


def _row_copy(src_hbm, dst_vmem, sem, src_row, dst_row):
    return pltpu.make_async_copy(src_hbm.at[pl.ds(src_row, 1)], dst_vmem.at[pl.ds(dst_row, 1)], sem)


def _gather_rows_kernel(idx_ref, src_ref, o_ref, buf_ref, sem):
    n = buf_ref.shape[0]
    base = pl.program_id(0) * n

    def issue(r, carry):
        _row_copy(src_ref, buf_ref, sem, idx_ref[base + r], r).start()
        return carry

    def wait(r, carry):
        _row_copy(src_ref, buf_ref, sem, 0, r).wait()
        return carry

    lax.fori_loop(0, n, issue, 0)
    lax.fori_loop(0, n, wait, 0)
    o_ref[...] = buf_ref[...].astype(o_ref.dtype)


def _gather_rows(src, idx, out_dtype):
    n = idx.shape[0]
    d = src.shape[1]
    tg = MOE_TILE
    return pl.pallas_call(
        _gather_rows_kernel,
        out_shape=jax.ShapeDtypeStruct((n, d), out_dtype),
        grid_spec=pltpu.PrefetchScalarGridSpec(
            num_scalar_prefetch=1,
            grid=(n // tg,),
            in_specs=[pl.BlockSpec(memory_space=pl.ANY)],
            out_specs=pl.BlockSpec((tg, d), lambda i, idx: (i, 0)),
            scratch_shapes=[pltpu.VMEM((tg, d), src.dtype), pltpu.SemaphoreType.DMA]),
        compiler_params=_params("arbitrary"),
        name="moe_gather",
    )(idx, src)


def _expert_up_kernel(te_ref, nt_ref, a_ref, wg_ref, wu_ref, o_ref, wgb_ref, wub_ref):
    i = pl.program_id(1)
    new_expert = jnp.logical_or(i == 0, te_ref[i] != te_ref[jnp.maximum(i - 1, 0)])

    @pl.when(new_expert)
    def _():
        wgb_ref[...] = wg_ref[0].astype(BF16)
        wub_ref[...] = wu_ref[0].astype(BF16)

    @pl.when(i < nt_ref[0])
    def _():
        a = a_ref[...]
        g = _dot(a, wgb_ref[...])
        u = _dot(a, wub_ref[...])
        o_ref[...] = (_silu(g) * u).astype(o_ref.dtype)

    @pl.when(i >= nt_ref[0])
    def _():
        o_ref[...] = jnp.zeros_like(o_ref)


def _expert_up(xs, tile_expert, n_tiles_used, w_gu, *, tn):
    p, k = xs.shape
    ff = w_gu.shape[2] // 2
    nj = ff // tn
    tm = MOE_TILE
    return pl.pallas_call(
        _expert_up_kernel,
        out_shape=jax.ShapeDtypeStruct((p, ff), BF16),
        grid_spec=pltpu.PrefetchScalarGridSpec(
            num_scalar_prefetch=2,
            grid=(nj, p // tm),
            in_specs=[pl.BlockSpec((tm, k), lambda j, i, te, nt: (i, 0)),
                      pl.BlockSpec((1, k, tn), lambda j, i, te, nt: (te[i], 0, j)),
                      pl.BlockSpec((1, k, tn), lambda j, i, te, nt: (te[i], 0, j + nj))],
            out_specs=pl.BlockSpec((tm, tn), lambda j, i, te, nt: (i, j)),
            scratch_shapes=[pltpu.VMEM((k, tn), BF16), pltpu.VMEM((k, tn), BF16)]),
        compiler_params=_params("parallel", "arbitrary"),
        name="moe_expert_up",
    )(tile_expert, n_tiles_used, xs, w_gu, w_gu)


def _expert_down_kernel(te_ref, nt_ref, a_ref, w_ref, o_ref, wb_ref):
    i = pl.program_id(1)
    new_expert = jnp.logical_or(i == 0, te_ref[i] != te_ref[jnp.maximum(i - 1, 0)])

    @pl.when(new_expert)
    def _():
        wb_ref[...] = w_ref[0].astype(BF16)

    @pl.when(i < nt_ref[0])
    def _():
        o_ref[...] = _dot(a_ref[...], wb_ref[...]).astype(o_ref.dtype)

    @pl.when(i >= nt_ref[0])
    def _():
        o_ref[...] = jnp.zeros_like(o_ref)


def _expert_down(act, tile_expert, n_tiles_used, w_down, *, tn):
    p, k = act.shape
    n = w_down.shape[2]
    tm = MOE_TILE
    return pl.pallas_call(
        _expert_down_kernel,
        out_shape=jax.ShapeDtypeStruct((p, n), F32),
        grid_spec=pltpu.PrefetchScalarGridSpec(
            num_scalar_prefetch=2,
            grid=(n // tn, p // tm),
            in_specs=[pl.BlockSpec((tm, k), lambda j, i, te, nt: (i, 0)),
                      pl.BlockSpec((1, k, tn), lambda j, i, te, nt: (te[i], 0, j))],
            out_specs=pl.BlockSpec((tm, tn), lambda j, i, te, nt: (i, j)),
            scratch_shapes=[pltpu.VMEM((k, tn), BF16)]),
        compiler_params=_params("parallel", "arbitrary"),
        name="moe_expert_down",
    )(tile_expert, n_tiles_used, act, w_down)


def _combine_ln_kernel(slot_ref, ys_ref, w_ref, x_ref, mg_ref, lnw_ref, lnb_ref, o_ref,
                       buf_ref, sem, *, alpha, gate_idx):
    n = x_ref.shape[0]
    n_tok = n * pl.num_programs(0)
    base = pl.program_id(0) * n

    def issue(r, carry):
        _row_copy(ys_ref, buf_ref.at[0], sem, slot_ref[base + r], r).start()
        _row_copy(ys_ref, buf_ref.at[1], sem, slot_ref[n_tok + base + r], r).start()
        return carry

    def wait(r, carry):
        _row_copy(ys_ref, buf_ref.at[0], sem, 0, r).wait()
        _row_copy(ys_ref, buf_ref.at[1], sem, 0, r).wait()
        return carry

    lax.fori_loop(0, n, issue, 0)
    lax.fori_loop(0, n, wait, 0)
    w = w_ref[...]
    f = w[:, 0:1] * buf_ref[0] + w[:, 1:2] * buf_ref[1]
    gate = mg_ref[0][gate_idx:gate_idx + 1]
    t = alpha * x_ref[...] + gate * f
    o_ref[...] = _layer_norm_rows(t, lnw_ref[...], lnb_ref[...])


def _combine_ln(ys, slots, weights, x, mod_gate, gate_idx, ln_w, ln_b, alpha, tiles_per_batch):
    rows, d = x.shape
    tc = ROW_TILE
    row_spec = pl.BlockSpec((tc, d), lambda r, s: (r, 0))
    vec_spec = pl.BlockSpec((1, d), lambda r, s: (0, 0))
    return pl.pallas_call(
        functools.partial(_combine_ln_kernel, alpha=alpha, gate_idx=gate_idx),
        out_shape=jax.ShapeDtypeStruct((rows, d), F32),
        grid_spec=pltpu.PrefetchScalarGridSpec(
            num_scalar_prefetch=1,
            grid=(rows // tc,),
            in_specs=[pl.BlockSpec(memory_space=pl.ANY),
                      pl.BlockSpec((tc, 2), lambda r, s: (r, 0)),
                      row_spec,
                      pl.BlockSpec((1, 6, d), lambda r, s: (r // tiles_per_batch, 0, 0)),
                      vec_spec, vec_spec],
            out_specs=row_spec,
            scratch_shapes=[pltpu.VMEM((2, tc, d), F32), pltpu.SemaphoreType.DMA]),
        compiler_params=_params("arbitrary"),
        name="moe_combine_layer_norm",
    )(slots, ys, weights, x, mod_gate, ln_w.reshape(1, d), ln_b.reshape(1, d))


def _routing_tables(top_i, n_slots):
    n_tok = top_i.shape[1]
    tm = MOE_TILE
    pair_e = top_i.T.reshape(-1)
    onehot = (pair_e[:, None] == jnp.arange(N_EXPERTS, dtype=jnp.int32)[None, :]).astype(jnp.int32)
    rank = jnp.cumsum(onehot, axis=0) - onehot
    counts = jnp.sum(onehot, axis=0)
    padded = ((counts + tm - 1) // tm) * tm
    ends = jnp.cumsum(padded)
    starts = ends - padded
    slot = jnp.sum(onehot * (starts[None, :] + rank), axis=1)
    token_of_slot = jnp.zeros((n_slots,), jnp.int32).at[slot].set(
        jnp.arange(2 * n_tok, dtype=jnp.int32) // 2)
    tile_start = jnp.arange(n_slots // tm, dtype=jnp.int32) * tm
    tile_expert = jnp.minimum(
        jnp.sum((tile_start[:, None] >= ends[None, :]).astype(jnp.int32), axis=1), N_EXPERTS - 1)
    n_tiles_used = (ends[-1:] // tm).astype(jnp.int32)
    slots = slot.reshape(n_tok, 2).T.reshape(-1)
    return slots, token_of_slot, tile_expert.astype(jnp.int32), n_tiles_used


def _moe(h_f32, router, w_gu, w_down):
    n_tok = h_f32.shape[0]
    top_i, top_w = _router(h_f32, router)
    n_slots = 2 * n_tok + N_EXPERTS * MOE_TILE
    slots, token_of_slot, tile_expert, n_tiles_used = _routing_tables(top_i, n_slots)
    xs = _gather_rows(h_f32, token_of_slot, BF16)
    act = _expert_up(xs, tile_expert, n_tiles_used, w_gu, tn=512)
    ys = _expert_down(act, tile_expert, n_tiles_used, w_down, tn=512)
    return ys, slots, top_w.T


def kernel(x, c, ctx, c_ctx, ada_w, ada_b, ln_w, ln_b, gdn_w_in, gdn_conv_w, gdn_a_log, gdn_dt_bias,
           gdn_norm_w, gdn_w_out, ffn_w_gu, ffn_w_down, diff_w_in, diff_lambda, diff_norm_w,
           diff_w_out, moe_router, moe_w_gu, moe_w_down):
    batch, n_lat, d = x.shape
    n_ctx = ctx.shape[1]
    depth = ada_w.shape[0]
    assert depth == 2 and n_ctx == ROW_TILE and n_lat % ROW_TILE == 0
    t_all = n_ctx + n_lat
    alpha = (2.0 * depth) ** 0.25
    tiles_all = t_all // ROW_TILE
    tiles_lat = n_lat // ROW_TILE

    n_cond = batch + 1
    cond = jnp.concatenate([c, c_ctx[None, :], jnp.zeros((-n_cond % 8, d), F32)], axis=0)
    mod = _ada_modulation(cond, ada_w, ada_b)[:, :n_cond].reshape(depth, n_cond, 6, d)

    xa = jnp.concatenate([ctx, x], axis=1).reshape(batch * t_all, d)

    h = _modulate(xa, mod[0], tiles_all, 1)
    y = _gdn_mixer(h, batch, gdn_w_in[0], gdn_conv_w[0], gdn_a_log[0], gdn_dt_bias[0],
                   gdn_norm_w[0], gdn_w_out[0])
    xa, h = _res_ln(xa, y, mod[0], 2, mod[0], 3, ln_w[0, 0], ln_b[0, 0], alpha, tiles_all, 1, BF16)
    act = _swiglu_up(h, ffn_w_gu[0], tm=1024, tn=512)
    f = _matmul(act, ffn_w_down[0], tm=512, tn=512, out_dtype=F32)
    xa, h = _res_ln(xa, f, mod[0], 5, mod[1], 0, ln_w[0, 1], ln_b[0, 1], alpha, tiles_all, 1, BF16)

    lam_init = 0.8 - 0.6 * math.exp(-0.3 * 1)
    qkv = _matmul(h, diff_w_in[0], tm=1024, tn=1024, out_dtype=BF16)
    cos, sin = _rope_tables(n_ctx, n_lat)
    qk = _rope(qkv, batch, cos, sin)
    o = _diff_attention(qk, qkv, batch, n_ctx, diff_lambda[0], diff_norm_w[0], lam_init)
    y = _matmul(o.reshape(batch * n_lat, DIFF_HEADS * DIFF_DV), diff_w_out[0],
                tm=1024, tn=1024, out_dtype=F32)
    xl = xa.reshape(batch, t_all, d)[:, n_ctx:].reshape(batch * n_lat, d)
    xl, hf = _res_ln(xl, y, mod[1], 2, mod[1], 3, ln_w[1, 0], ln_b[1, 0], alpha, tiles_lat, 0, F32)
    ys, slots, top_w = _moe(hf, moe_router[0], moe_w_gu[0], moe_w_down[0])
    out = _combine_ln(ys, slots, top_w, xl, mod[1], 5, ln_w[1, 1], ln_b[1, 1], alpha, tiles_lat)
    return out.reshape(batch, n_lat, d)
```

```python
import functools
import math

import jax
import jax.numpy as jnp
from jax import lax
from jax.experimental import pallas as pl
from jax.experimental.pallas import tpu as pltpu

F32 = jnp.float32
BF16 = jnp.bfloat16

GDN_QK_HEADS = 16
GDN_V_HEADS = 32
GDN_DK = 128
GDN_DV = 128
GDN_CONV = 5
GDN_CHUNK = 64
GDN_Q = GDN_QK_HEADS * GDN_DK
GDN_V = GDN_V_HEADS * GDN_DV
GDN_QKV = 2 * GDN_Q + GDN_V
DIFF_HEADS = 8
DIFF_DH = 128
DIFF_DV = 2 * DIFF_DH
DIFF_Q = DIFF_HEADS * 2 * DIFF_DH
GRID_W = 64
ROPE_BASE = 10000.0
ROPE_AXIS_DIM = DIFF_DH // 2
N_EXPERTS = 8
LN_EPS = 1e-5
RMS_EPS = 1e-6

V7X_VMEM_BYTES = 64 * 1024 * 1024
VMEM_LIMIT = V7X_VMEM_BYTES - 8 * 1024 * 1024
LANES = 128
ROW_TILE = 256
CHUNKS_PER_TILE = ROW_TILE // GDN_CHUNK
MOE_TILE = 512


def _params(*sem):
    return pltpu.CompilerParams(dimension_semantics=sem, vmem_limit_bytes=VMEM_LIMIT)


def _row_tile(m, preferred):
    t = preferred
    while m % t and t > ROW_TILE:
        t //= 2
    assert m % t == 0
    return t


def _dot(a, b):
    return jnp.dot(a, b, preferred_element_type=F32)


def _dot_nt(a, b):
    return lax.dot_general(a, b, (((1,), (1,)), ((), ())), preferred_element_type=F32)


def _dot_tn(a, b):
    return lax.dot_general(a, b, (((0,), (0,)), ((), ())), preferred_element_type=F32)


def _silu(x):
    return x * jax.nn.sigmoid(x)


def _ada_kernel(c_ref, w_ref, b_ref, o_ref):
    cond = _silu(c_ref[...])
    o_ref[0] = jnp.dot(cond, w_ref[0], preferred_element_type=F32,
                       precision=lax.Precision.HIGHEST) + b_ref[0]


def _ada_modulation(cond, ada_w, ada_b):
    depth, d, n = ada_w.shape
    rows = cond.shape[0]
    tn = 1024
    return pl.pallas_call(
        _ada_kernel,
        out_shape=jax.ShapeDtypeStruct((depth, rows, n), F32),
        grid=(depth, n // tn),
        in_specs=[pl.BlockSpec((rows, d), lambda l, j: (0, 0)),
                  pl.BlockSpec((1, d, tn), lambda l, j: (l, 0, j)),
                  pl.BlockSpec((1, 1, tn), lambda l, j: (l, 0, j))],
        out_specs=pl.BlockSpec((1, rows, tn), lambda l, j: (l, 0, j)),
        compiler_params=_params("parallel", "parallel"),
        name="ada_modulation",
    )(cond, ada_w, ada_b.reshape(depth, 1, n))


def _mod_index(r, tiles_per_batch, ctx_tiles, ctx_row):
    return jnp.where(r % tiles_per_batch < ctx_tiles, ctx_row, r // tiles_per_batch)


def _modulate_kernel(x_ref, m_ref, o_ref):
    m = m_ref[0]
    o_ref[...] = (x_ref[...] * (1.0 + m[1:2]) + m[0:1]).astype(o_ref.dtype)


def _modulate(x, mod, tiles_per_batch, ctx_tiles):
    rows, d = x.shape
    ctx_row = mod.shape[0] - 1
    midx = functools.partial(_mod_index, tiles_per_batch=tiles_per_batch,
                             ctx_tiles=ctx_tiles, ctx_row=ctx_row)
    return pl.pallas_call(
        _modulate_kernel,
        out_shape=jax.ShapeDtypeStruct((rows, d), BF16),
        grid=(rows // ROW_TILE,),
        in_specs=[pl.BlockSpec((ROW_TILE, d), lambda r: (r, 0)),
                  pl.BlockSpec((1, 6, d), lambda r: (midx(r), 0, 0))],
        out_specs=pl.BlockSpec((ROW_TILE, d), lambda r: (r, 0)),
        compiler_params=_params("parallel"),
        name="modulate",
    )(x, mod)


def _layer_norm_rows(t, w, b):
    mu = jnp.mean(t, axis=-1, keepdims=True)
    tc = t - mu
    var = jnp.mean(tc * tc, axis=-1, keepdims=True)
    return tc * lax.rsqrt(var + LN_EPS) * w + b


def _res_ln_kernel(x_ref, y_ref, mg_ref, mn_ref, lnw_ref, lnb_ref, xo_ref, ho_ref,
                   *, alpha, gate_idx, shift_idx):
    gate = mg_ref[0][gate_idx:gate_idx + 1]
    t = alpha * x_ref[...] + gate * y_ref[...].astype(F32)
    xn = _layer_norm_rows(t, lnw_ref[...], lnb_ref[...])
    xo_ref[...] = xn
    mn = mn_ref[0]
    h = xn * (1.0 + mn[shift_idx + 1:shift_idx + 2]) + mn[shift_idx:shift_idx + 1]
    ho_ref[...] = h.astype(ho_ref.dtype)


def _res_ln(x, y, mod_gate, gate_idx, mod_next, shift_idx, ln_w, ln_b, alpha,
            tiles_per_batch, ctx_tiles, h_dtype):
    rows, d = x.shape
    ctx_row = mod_gate.shape[0] - 1
    midx = functools.partial(_mod_index, tiles_per_batch=tiles_per_batch,
                             ctx_tiles=ctx_tiles, ctx_row=ctx_row)
    row_spec = pl.BlockSpec((ROW_TILE, d), lambda r: (r, 0))
    mod_spec = pl.BlockSpec((1, 6, d), lambda r: (midx(r), 0, 0))
    vec_spec = pl.BlockSpec((1, d), lambda r: (0, 0))
    return pl.pallas_call(
        functools.partial(_res_ln_kernel, alpha=alpha, gate_idx=gate_idx, shift_idx=shift_idx),
        out_shape=(jax.ShapeDtypeStruct((rows, d), F32), jax.ShapeDtypeStruct((rows, d), h_dtype)),
        grid=(rows // ROW_TILE,),
        in_specs=[row_spec, row_spec, mod_spec, mod_spec, vec_spec, vec_spec],
        out_specs=(row_spec, row_spec),
        compiler_params=_params("parallel"),
        name="residual_layer_norm",
    )(x, y, mod_gate, mod_next, ln_w.reshape(1, d), ln_b.reshape(1, d))


def _mm_kernel(a_ref, w_ref, o_ref, wbf_ref):
    @pl.when(pl.program_id(1) == 0)
    def _():
        wbf_ref[...] = w_ref[...].astype(BF16)

    o_ref[...] = _dot(a_ref[...], wbf_ref[...]).astype(o_ref.dtype)


def _matmul(a, w, *, tm, tn, out_dtype, col_start=0, n_cols=None):
    m, k = a.shape
    tm = _row_tile(m, tm)
    n_cols = w.shape[1] - col_start if n_cols is None else n_cols
    assert m % tm == 0 and n_cols % tn == 0 and col_start % tn == 0
    off = col_start // tn
    return pl.pallas_call(
        _mm_kernel,
        out_shape=jax.ShapeDtypeStruct((m, n_cols), out_dtype),
        grid=(n_cols // tn, m // tm),
        in_specs=[pl.BlockSpec((tm, k), lambda j, i: (i, 0)),
                  pl.BlockSpec((k, tn), lambda j, i: (0, j + off))],
        out_specs=pl.BlockSpec((tm, tn), lambda j, i: (i, j)),
        scratch_shapes=[pltpu.VMEM((k, tn), BF16)],
        compiler_params=_params("parallel", "arbitrary"),
        name="matmul",
    )(a, w)


def _swiglu_up_kernel(a_ref, wg_ref, wu_ref, o_ref, wgb_ref, wub_ref):
    @pl.when(pl.program_id(1) == 0)
    def _():
        wgb_ref[...] = wg_ref[...].astype(BF16)
        wub_ref[...] = wu_ref[...].astype(BF16)

    a = a_ref[...]
    g = _dot(a, wgb_ref[...])
    u = _dot(a, wub_ref[...])
    o_ref[...] = (_silu(g) * u).astype(o_ref.dtype)


def _swiglu_up(a, w_gu, *, tm, tn):
    m, k = a.shape
    tm = _row_tile(m, tm)
    ff = w_gu.shape[1] // 2
    assert m % tm == 0 and ff % tn == 0
    nj = ff // tn
    return pl.pallas_call(
        _swiglu_up_kernel,
        out_shape=jax.ShapeDtypeStruct((m, ff), BF16),
        grid=(nj, m // tm),
        in_specs=[pl.BlockSpec((tm, k), lambda j, i: (i, 0)),
                  pl.BlockSpec((k, tn), lambda j, i: (0, j)),
                  pl.BlockSpec((k, tn), lambda j, i: (0, j + nj))],
        out_specs=pl.BlockSpec((tm, tn), lambda j, i: (i, j)),
        scratch_shapes=[pltpu.VMEM((k, tn), BF16), pltpu.VMEM((k, tn), BF16)],
        compiler_params=_params("parallel", "arbitrary"),
        name="swiglu_up",
    )(a, w_gu, w_gu)


def _gdn_conv_kernel(cur_ref, prev_ref, next_ref, w_ref, o_ref, xe_ref, *, n_qk_blocks, q_scale):
    i = pl.program_id(1)
    j = pl.program_id(2)
    n_tiles = pl.num_programs(1)
    prev_ok = (i >= 2).astype(F32)
    next_ok = jnp.logical_and(i >= 1, i < n_tiles - 1).astype(F32)
    xe_ref[0:8, :] = prev_ref[0].astype(F32)[8:16] * prev_ok
    xe_ref[8:8 + ROW_TILE, :] = cur_ref[0].astype(F32)
    xe_ref[8 + ROW_TILE:16 + ROW_TILE, :] = next_ref[0].astype(F32)[0:8] * next_ok
    pad = (GDN_CONV - 1) // 2
    is_qk = j < n_qk_blocks
    scale = jnp.where(j < n_qk_blocks // 2, q_scale, 1.0)
    for g in range(xe_ref.shape[1] // GDN_DK):
        sl = slice(g * GDN_DK, (g + 1) * GDN_DK)
        xe = xe_ref[:, sl]
        acc = None
        for t in range(GDN_CONV):
            shifted = xe if t == pad else pltpu.roll(xe, (pad - t) % xe.shape[0], axis=0)
            term = shifted[8:8 + ROW_TILE] * w_ref[t:t + 1, sl]
            acc = term if acc is None else acc + term
        y = _silu(acc)
        inv = lax.rsqrt(jnp.sum(y * y, axis=-1, keepdims=True) + 1e-6) * scale
        o_ref[0, :, sl] = (y * jnp.where(is_qk, inv, 1.0)).astype(o_ref.dtype)


def _gdn_conv(pqz, conv_w, batch):
    rows, _ = pqz.shape
    t_all = rows // batch
    x = pqz.reshape(batch, t_all, pqz.shape[1])
    cb = 512
    n_tiles = t_all // ROW_TILE
    halo = 16
    last_halo = t_all // halo - 1
    per_tile = ROW_TILE // halo
    return pl.pallas_call(
        functools.partial(_gdn_conv_kernel, n_qk_blocks=2 * GDN_Q // cb, q_scale=GDN_DK ** -0.5),
        out_shape=jax.ShapeDtypeStruct((batch, t_all, GDN_QKV), BF16),
        grid=(batch, n_tiles, GDN_QKV // cb),
        in_specs=[pl.BlockSpec((1, ROW_TILE, cb), lambda b, i, j: (b, i, j)),
                  pl.BlockSpec((1, halo, cb), lambda b, i, j: (b, jnp.maximum(i * per_tile - 1, 0), j)),
                  pl.BlockSpec((1, halo, cb),
                               lambda b, i, j: (b, jnp.minimum((i + 1) * per_tile, last_halo), j)),
                  pl.BlockSpec((GDN_CONV, cb), lambda b, i, j: (0, j))],
        out_specs=pl.BlockSpec((1, ROW_TILE, cb), lambda b, i, j: (b, i, j)),
        scratch_shapes=[pltpu.VMEM((ROW_TILE + 16, cb), F32)],
        compiler_params=_params("parallel", "parallel", "parallel"),
        name="gdn_conv",
    )(x, x, x, conv_w)


def _gdn_gate_kernel(ab_ref, alog_ref, dtb_ref, o_ref):
    ab = ab_ref[...]
    half = LANES // 2
    quarter = LANES // 4
    lane = lax.broadcasted_iota(jnp.int32, ab.shape, 1)
    beta = jax.nn.sigmoid(ab)
    x = ab + dtb_ref[...]
    softplus = jnp.maximum(x, 0.0) + jnp.log1p(jnp.exp(-jnp.abs(x)))
    g = jnp.where(lane >= half, -jnp.exp(alog_ref[...]) * softplus, 0.0)
    g = pltpu.roll(g, half, axis=1)
    n = ab.shape[0]
    ri = lax.broadcasted_iota(jnp.int32, (n, n), 0)
    ci = lax.broadcasted_iota(jnp.int32, (n, n), 1)
    same = (ri // GDN_CHUNK) == (ci // GDN_CHUNK)
    hi = lax.Precision.HIGHEST
    m_fwd = jnp.logical_and(same, ci <= ri).astype(F32)
    m_bwd = jnp.logical_and(same, ci >= ri).astype(F32)
    gc_f = jnp.dot(m_fwd, g, preferred_element_type=F32, precision=hi)
    gc_b = jnp.dot(m_bwd, g, preferred_element_type=F32, precision=hi)
    tot = jnp.dot(same.astype(F32), g, preferred_element_type=F32, precision=hi)
    gc = jnp.where(lane < quarter, gc_f, gc_b)
    eg = jnp.exp(gc)
    ek = jnp.exp(tot - gc)
    for d in range(2):
        parts = [pltpu.roll(val, (q * quarter - d * quarter) % LANES, axis=1)
                 for q, val in enumerate((beta, gc, eg, ek))]
        o_ref[d] = jnp.where(lane < quarter, parts[0],
                             jnp.where(lane < 2 * quarter, parts[1],
                                       jnp.where(lane < 3 * quarter, parts[2], parts[3])))


def _gdn_gates(ab, a_log, dt_bias):
    rows = ab.shape[0]
    zeros = jnp.zeros((LANES // 2,), F32)
    alog = jnp.concatenate([zeros, a_log.reshape(-1).astype(F32)]).reshape(1, LANES)
    dtb = jnp.concatenate([zeros, dt_bias.reshape(-1).astype(F32)]).reshape(1, LANES)
    return pl.pallas_call(
        _gdn_gate_kernel,
        out_shape=jax.ShapeDtypeStruct((2, rows, LANES), F32),
        grid=(rows // ROW_TILE,),
        in_specs=[pl.BlockSpec((ROW_TILE, LANES), lambda r: (r, 0)),
                  pl.BlockSpec((1, LANES), lambda r: (0, 0)),
                  pl.BlockSpec((1, LANES), lambda r: (0, 0))],
        out_specs=pl.BlockSpec((2, ROW_TILE, LANES), lambda r: (0, r, 0)),
        compiler_params=_params("parallel"),
        name="gdn_gates",
    )(ab, alog, dtb)


def _block_masks(n):
    ri = lax.broadcasted_iota(jnp.int32, (n, n), 0)
    ci = lax.broadcasted_iota(jnp.int32, (n, n), 1)
    masks = [(ri >> 1) == (ci >> 1)]
    for level in range(1, int(math.log2(n))):
        masks.append(jnp.logical_and((ri >> (level + 1)) == (ci >> (level + 1)),
                                     (ri >> level) != (ci >> level)))
    return masks


def _bmm(a, b):
    return jnp.einsum('hij,hjk->hik', a, b, preferred_element_type=F32)


def _bmm_nt(a, b):
    return jnp.einsum('hik,hjk->hij', a, b, preferred_element_type=F32)


def _bmm_tn(a, b):
    return jnp.einsum('hki,hkj->hij', a, b, preferred_element_type=F32)


def _unit_triangular_inverse(a, eye, masks):
    d = eye - jnp.where(masks[0], a, 0.0)
    for mask in masks[1:]:
        db = d.astype(BF16)
        ad = _bmm(jnp.where(mask, a, 0.0).astype(BF16), db)
        d = d - _bmm(db, ad.astype(BF16))
    return d


def _gdn_scan_kernel(q_ref, k_ref, v_ref, g_ref, gr_ref, o_ref, s_ref):
    d = pl.program_id(1)
    s = pl.program_id(2)

    @pl.when(s == 0)
    def _():
        s_ref[...] = jnp.zeros_like(s_ref)

    c = GDN_CHUNK
    n_h = GDN_V_HEADS
    rep = GDN_V_HEADS // GDN_QK_HEADS
    rev = d == 1
    ri = lax.broadcasted_iota(jnp.int32, (c, c), 0)
    ci = lax.broadcasted_iota(jnp.int32, (c, c), 1)
    later = jnp.where(rev, ci, ri)
    earlier = jnp.where(rev, ri, ci)
    incl = later >= earlier
    strict = later > earlier
    eye = (ri == ci).astype(F32)
    masks = _block_masks(c)

    def chunk_step(step, carry):
        cc = jnp.where(rev, CHUNKS_PER_TILE - 1 - step, step)
        off = pl.multiple_of(cc * c, c)
        q_all = q_ref[0, pl.ds(off, c), :]
        k_all = k_ref[0, pl.ds(off, c), :]
        v_all = v_ref[0, pl.ds(off, c), :]
        gates = g_ref[0, pl.ds(off, c), :]
        gc_rows = gr_ref[0, 0, cc]
        qs = [q_all[:, i * GDN_DK:(i + 1) * GDN_DK] for i in range(GDN_QK_HEADS)]
        ks = [k_all[:, i * GDN_DK:(i + 1) * GDN_DK] for i in range(GDN_QK_HEADS)]
        k_qk = jnp.stack(ks)
        kk_raw = _bmm_nt(k_qk, k_qk)
        qk_raw = _bmm_nt(jnp.stack(qs), k_qk)

        def col(quantity, h):
            lane = quantity * n_h + h
            return gates[:, lane:lane + 1]

        a_l, qk_l, rhs_l, qg_l, kt_l, egl_l = [], [], [], [], [], []
        for h in range(n_h):
            beta, gc, eg, ek = (col(i, h) for i in range(4))
            gdiff = gc - gc_rows[h:h + 1, :]
            decay = jnp.where(incl, jnp.exp(jnp.where(incl, gdiff, 0.0)), 0.0)
            a_l.append(jnp.where(strict, kk_raw[h // rep] * (beta * decay), 0.0))
            qk_l.append((qk_raw[h // rep] * decay).astype(BF16))
            kf = ks[h // rep].astype(F32)
            vf = v_all[:, h * GDN_DV:(h + 1) * GDN_DV].astype(F32)
            rhs_l.append(jnp.concatenate([vf * beta, kf * (beta * eg)], axis=1).astype(BF16))
            eg_b = jnp.broadcast_to(eg, (c, GDN_DK))
            qg_l.append((qs[h // rep].astype(F32) * eg_b).astype(BF16))
            kt_l.append((kf * ek).astype(BF16))
            egl_l.append(jnp.where(rev, eg_b[0:1, :], eg_b[c - 1:c, :]))
        t_inv = _unit_triangular_inverse(jnp.stack(a_l), eye, masks)
        uw = _bmm(t_inv.astype(BF16), jnp.stack(rhs_l))
        u = uw[:, :, 0:GDN_DV]
        w = uw[:, :, GDN_DV:]
        state = s_ref[...]
        state_b = state.astype(BF16)
        v_new = u - _bmm(w.astype(BF16), state_b)
        v_new_b = v_new.astype(BF16)
        o = _bmm(jnp.stack(qg_l), state_b) + _bmm(jnp.stack(qk_l), v_new_b)
        s_ref[...] = state * jnp.stack(egl_l) + _bmm_tn(jnp.stack(kt_l), v_new_b)
        for h in range(n_h):
            o_ref[0, 0, pl.ds(off, c), h * GDN_DV:(h + 1) * GDN_DV] = o[h].astype(o_ref.dtype)
        return carry

    lax.fori_loop(0, CHUNKS_PER_TILE, chunk_step, 0)


def _gdn_scan(qkv, gates, gc_rows, batch):
    _, t_all, _ = qkv.shape
    n_tiles = t_all // ROW_TILE

    def tile(d, s):
        return jnp.where(d == 0, s, jnp.where(s == 0, 0, n_tiles - s))

    return pl.pallas_call(
        _gdn_scan_kernel,
        out_shape=jax.ShapeDtypeStruct((2, batch, t_all, GDN_V), BF16),
        grid=(batch, 2, n_tiles),
        in_specs=[
            pl.BlockSpec((1, ROW_TILE, GDN_Q), lambda b, d, s: (b, tile(d, s), 0)),
            pl.BlockSpec((1, ROW_TILE, GDN_Q), lambda b, d, s: (b, tile(d, s), 1)),
            pl.BlockSpec((1, ROW_TILE, GDN_V), lambda b, d, s: (b, tile(d, s), 1)),
            pl.BlockSpec((1, ROW_TILE, LANES), lambda b, d, s: (d, b * n_tiles + tile(d, s), 0)),
            pl.BlockSpec((1, 1, CHUNKS_PER_TILE, GDN_V_HEADS, GDN_CHUNK),
                         lambda b, d, s: (b, d, tile(d, s), 0, 0)),
        ],
        out_specs=pl.BlockSpec((1, 1, ROW_TILE, GDN_V), lambda b, d, s: (d, b, tile(d, s), 0)),
        scratch_shapes=[pltpu.VMEM((GDN_V_HEADS, GDN_DK, GDN_DV), F32)],
        compiler_params=_params("parallel", "arbitrary", "arbitrary"),
        name="gdn_scan",
    )(qkv, qkv, qkv, gates, gc_rows)


def _gdn_norm_gate_kernel(of_ref, ob_ref, z_ref, w_ref, o_ref):
    w = w_ref[...]
    for g in range(o_ref.shape[1] // GDN_DV):
        sl = slice(g * GDN_DV, (g + 1) * GDN_DV)
        og = of_ref[0, :, sl].astype(F32) + ob_ref[0, :, sl].astype(F32)
        inv = lax.rsqrt(jnp.mean(og * og, axis=-1, keepdims=True) + RMS_EPS)
        o_ref[:, sl] = (og * inv * w * _silu(z_ref[:, sl].astype(F32))).astype(o_ref.dtype)


def _gdn_norm_gate(o2, pqz, norm_w):
    _, rows, width = o2.shape
    cb = width
    z_off = GDN_QKV // cb
    return pl.pallas_call(
        _gdn_norm_gate_kernel,
        out_shape=jax.ShapeDtypeStruct((rows, width), BF16),
        grid=(rows // ROW_TILE, width // cb),
        in_specs=[pl.BlockSpec((1, ROW_TILE, cb), lambda r, j: (0, r, j)),
                  pl.BlockSpec((1, ROW_TILE, cb), lambda r, j: (1, r, j)),
                  pl.BlockSpec((ROW_TILE, cb), lambda r, j: (r, z_off + j)),
                  pl.BlockSpec((1, GDN_DV), lambda r, j: (0, 0))],
        out_specs=pl.BlockSpec((ROW_TILE, cb), lambda r, j: (r, j)),
        compiler_params=_params("parallel", "parallel"),
        name="gdn_norm_gate",
    )(o2, o2, pqz, norm_w.reshape(1, GDN_DV))


def _gdn_mixer(h, batch, w_in, conv_w, a_log, dt_bias, norm_w, w_out):
    rows = h.shape[0]
    t_all = rows // batch
    pqz = _matmul(h, w_in, tm=1024, tn=1024, out_dtype=BF16, col_start=0, n_cols=GDN_QKV + GDN_V)
    ab = _matmul(h, w_in, tm=1024, tn=LANES, out_dtype=F32, col_start=GDN_QKV + GDN_V, n_cols=LANES)
    gates = _gdn_gates(ab, a_log, dt_bias)
    n_chunks = t_all // GDN_CHUNK
    gc_rows = gates[:, :, GDN_V_HEADS:2 * GDN_V_HEADS]
    gc_rows = gc_rows.reshape(2, batch, n_chunks, GDN_CHUNK, GDN_V_HEADS).transpose(1, 0, 2, 4, 3)
    qkv = _gdn_conv(pqz, conv_w, batch)
    o2 = _gdn_scan(qkv, gates, gc_rows, batch)
    og = _gdn_norm_gate(o2.reshape(2, rows, GDN_V), pqz, norm_w)
    return _matmul(og, w_out, tm=1024, tn=512, out_dtype=F32)


def _rope_tables(n_ctx, n_lat):
    rows = n_lat // GRID_W
    row = jnp.repeat(jnp.arange(rows, dtype=F32), GRID_W)
    col = jnp.tile(jnp.arange(GRID_W, dtype=F32), rows)
    inv_freq = 1.0 / (ROPE_BASE ** (jnp.arange(0, ROPE_AXIS_DIM, 2, dtype=F32) / ROPE_AXIS_DIM))
    ang_r = row[:, None] * inv_freq
    ang_c = col[:, None] * inv_freq
    cos = jnp.concatenate([jnp.cos(ang_r), jnp.cos(ang_r), jnp.cos(ang_c), jnp.cos(ang_c)], axis=1)
    sin = jnp.concatenate([-jnp.sin(ang_r), jnp.sin(ang_r), -jnp.sin(ang_c), jnp.sin(ang_c)], axis=1)
    cos = jnp.concatenate([jnp.ones((n_ctx, DIFF_DH), F32), cos], axis=0)
    sin = jnp.concatenate([jnp.zeros((n_ctx, DIFF_DH), F32), sin], axis=0)
    return cos, sin


def _qkv_rope_kernel(a_ref, w_ref, cos_ref, sin_ref, o_ref, wbf_ref, *, n_rope_blocks, n_q_blocks,
                     q_scale):
    j = pl.program_id(0)

    @pl.when(pl.program_id(1) == 0)
    def _():
        wbf_ref[...] = w_ref[...].astype(BF16)

    acc = _dot(a_ref[...], wbf_ref[...])

    @pl.when(j >= n_rope_blocks)
    def _():
        o_ref[...] = acc.astype(o_ref.dtype)

    @pl.when(j < n_rope_blocks)
    def _():
        scale = jnp.where(j < n_q_blocks, q_scale, 1.0)
        cos = cos_ref[...] * scale
        sin = sin_ref[...] * scale
        quarter = ROPE_AXIS_DIM // 2
        lane = lax.broadcasted_iota(jnp.int32, cos.shape, 1)
        first = (lane % ROPE_AXIS_DIM) < quarter
        for g in range(acc.shape[1] // DIFF_DH):
            sl = slice(g * DIFF_DH, (g + 1) * DIFF_DH)
            xg = acc[:, sl]
            partner = jnp.where(first, pltpu.roll(xg, DIFF_DH - quarter, axis=1),
                                pltpu.roll(xg, quarter, axis=1))
            o_ref[:, sl] = (xg * cos + partner * sin).astype(o_ref.dtype)


def _qkv_rope(a, w, cos, sin, *, tm, tn, q_scale):
    m, k = a.shape
    tm = _row_tile(m, tm)
    n = w.shape[1]
    assert n % tn == 0 and DIFF_Q % tn == 0
    return pl.pallas_call(
        functools.partial(_qkv_rope_kernel, n_rope_blocks=2 * DIFF_Q // tn, n_q_blocks=DIFF_Q // tn,
                          q_scale=q_scale),
        out_shape=jax.ShapeDtypeStruct((m, n), BF16),
        grid=(n // tn, m // tm),
        in_specs=[pl.BlockSpec((tm, k), lambda j, i: (i, 0)),
                  pl.BlockSpec((k, tn), lambda j, i: (0, j)),
                  pl.BlockSpec((tm, DIFF_DH), lambda j, i: (i, 0)),
                  pl.BlockSpec((tm, DIFF_DH), lambda j, i: (i, 0))],
        out_specs=pl.BlockSpec((tm, tn), lambda j, i: (i, j)),
        scratch_shapes=[pltpu.VMEM((k, tn), BF16)],
        compiler_params=_params("parallel", "arbitrary"),
        name="qkv_rope",
    )(a, w, cos, sin)


def _diff_attn_kernel(lam_ref, nw_ref, q_ref, k_ref, v_ref, o_ref, *, lam_init):
    lp = lam_ref[...]
    lam = (jnp.exp(jnp.sum(lp[0:1] * lp[1:2], axis=-1, keepdims=True))
           - jnp.exp(jnp.sum(lp[2:3] * lp[3:4], axis=-1, keepdims=True)) + lam_init)
    q = q_ref[0]
    k = k_ref[0]
    v = v_ref[0]

    def attend(m):
        sl = slice(m * DIFF_DH, (m + 1) * DIFF_DH)
        s2 = _dot_nt(q[:, sl], k[:, sl])
        p = jnp.exp2(s2 - jnp.max(s2, axis=-1, keepdims=True))
        return _dot(p.astype(BF16), v), jnp.sum(p, axis=-1, keepdims=True)

    o1, l1 = attend(0)
    o2, l2 = attend(1)
    o = o1 * (1.0 / l1) - o2 * (lam / l2)
    inv = lax.rsqrt(jnp.mean(o * o, axis=-1, keepdims=True) + RMS_EPS)
    o_ref[0] = (o * inv * nw_ref[...] * (1.0 - lam_init)).astype(o_ref.dtype)


def _diff_attention(qkv, batch, n_ctx, lam_p, norm_w, lam_init):
    t_all = qkv.shape[0] // batch
    n_lat = t_all - n_ctx
    tq = ROW_TILE
    assert n_ctx % tq == 0
    ctx_tiles = n_ctx // tq
    v3 = qkv.reshape(batch, t_all, qkv.shape[1])
    k_off = DIFF_Q // DIFF_DV
    v_off = 2 * DIFF_Q // DIFF_DV
    return pl.pallas_call(
        functools.partial(_diff_attn_kernel, lam_init=lam_init),
        out_shape=jax.ShapeDtypeStruct((batch, n_lat, DIFF_HEADS * DIFF_DV), BF16),
        grid=(batch, DIFF_HEADS, n_lat // tq),
        in_specs=[pl.BlockSpec((4, DIFF_DH), lambda b, h, i: (0, 0)),
                  pl.BlockSpec((1, DIFF_DV), lambda b, h, i: (0, 0)),
                  pl.BlockSpec((1, tq, DIFF_DV), lambda b, h, i: (b, i + ctx_tiles, h)),
                  pl.BlockSpec((1, t_all, DIFF_DV), lambda b, h, i: (b, 0, k_off + h)),
                  pl.BlockSpec((1, t_all, DIFF_DV), lambda b, h, i: (b, 0, v_off + h))],
        out_specs=pl.BlockSpec((1, tq, DIFF_DV), lambda b, h, i: (b, i, h)),
        compiler_params=_params("parallel", "parallel", "arbitrary"),
        name="diff_attention",
    )(lam_p.astype(F32), norm_w.reshape(1, DIFF_DV), v3, v3, v3)


def _router_kernel(h_ref, r_ref, idx_ref, w_ref):
    logits = _dot_nt(r_ref[...].astype(BF16), h_ref[...].astype(BF16))
    e = lax.broadcasted_iota(jnp.int32, logits.shape, 0)
    n_e = logits.shape[0]
    m1 = jnp.max(logits, axis=0, keepdims=True)
    i1 = jnp.min(jnp.where(logits == m1, e, n_e), axis=0, keepdims=True)
    rest = jnp.where(e == i1, -jnp.inf, logits)
    m2 = jnp.max(rest, axis=0, keepdims=True)
    i2 = jnp.min(jnp.where(rest == m2, e, n_e), axis=0, keepdims=True)
    t = jnp.exp(m2 - m1)
    idx_ref[0:1, :] = i1
    idx_ref[1:2, :] = i2
    w_ref[0:1, :] = 1.0 / (1.0 + t)
    w_ref[1:2, :] = t / (1.0 + t)


def _router(h, router):
    rows, d = h.shape
    tm = _row_tile(rows, 1024)
    return pl.pallas_call(
        _router_kernel,
        out_shape=(jax.ShapeDtypeStruct((2, rows), jnp.int32), jax.ShapeDtypeStruct((2, rows), F32)),
        grid=(rows // tm,),
        in_specs=[pl.BlockSpec((tm, d), lambda i: (i, 0)),
                  pl.BlockSpec((N_EXPERTS, d), lambda i: (0, 0))],
        out_specs=(pl.BlockSpec((2, tm), lambda i: (0, i)), pl.BlockSpec((2, tm), lambda i: (0, i))),
        compiler_params=_params("parallel"),
        name="moe_router",
    )(h, router.T)


def _row_copy(src_hbm, dst_vmem, sem, src_row, dst_row):
    return pltpu.make_async_copy(src_hbm.at[pl.ds(src_row, 1)], dst_vmem.at[pl.ds(dst_row, 1)], sem)


def _gather_rows_kernel(idx_ref, src_ref, o_ref, buf_ref, sem):
    n = o_ref.shape[0]
    i = pl.program_id(0)
    slot = i % 2

    def issue_tile(tile, dst_slot):
        def body(r, carry):
            _row_copy(src_ref, buf_ref.at[dst_slot], sem.at[dst_slot], idx_ref[tile * n + r], r).start()
            return carry
        lax.fori_loop(0, n, body, 0, unroll=8)

    @pl.when(i == 0)
    def _():
        issue_tile(0, 0)

    @pl.when(i + 1 < pl.num_programs(0))
    def _():
        issue_tile(i + 1, 1 - slot)

    def wait_row(r, carry):
        _row_copy(src_ref, buf_ref.at[slot], sem.at[slot], 0, r).wait()
        return carry

    lax.fori_loop(0, n, wait_row, 0, unroll=8)
    o_ref[...] = buf_ref[slot].astype(o_ref.dtype)


def _gather_rows(src, idx, out_dtype):
    n = idx.shape[0]
    d = src.shape[1]
    tg = ROW_TILE
    return pl.pallas_call(
        _gather_rows_kernel,
        out_shape=jax.ShapeDtypeStruct((n, d), out_dtype),
        grid_spec=pltpu.PrefetchScalarGridSpec(
            num_scalar_prefetch=1,
            grid=(n // tg,),
            in_specs=[pl.BlockSpec(memory_space=pl.ANY)],
            out_specs=pl.BlockSpec((tg, d), lambda i, idx: (i, 0)),
            scratch_shapes=[pltpu.VMEM((2, tg, d), src.dtype), pltpu.SemaphoreType.DMA((2,))]),
        compiler_params=_params("arbitrary"),
        name="moe_gather",
    )(idx, src)


def _expert_up_kernel(te_ref, nt_ref, a_ref, wg_ref, wu_ref, o_ref, wgb_ref, wub_ref):
    i = pl.program_id(1)
    new_expert = jnp.logical_or(i == 0, te_ref[i] != te_ref[jnp.maximum(i - 1, 0)])

    @pl.when(new_expert)
    def _():
        wgb_ref[...] = wg_ref[0].astype(BF16)
        wub_ref[...] = wu_ref[0].astype(BF16)

    @pl.when(i < nt_ref[0])
    def _():
        a = a_ref[...]
        g = _dot(a, wgb_ref[...])
        u = _dot(a, wub_ref[...])
        o_ref[...] = (_silu(g) * u).astype(o_ref.dtype)

    @pl.when(i >= nt_ref[0])
    def _():
        o_ref[...] = jnp.zeros_like(o_ref)


def _expert_up(xs, tile_expert, n_tiles_used, w_gu, *, tn):
    p, k = xs.shape
    ff = w_gu.shape[2] // 2
    nj = ff // tn
    tm = MOE_TILE
    return pl.pallas_call(
        _expert_up_kernel,
        out_shape=jax.ShapeDtypeStruct((p, ff), BF16),
        grid_spec=pltpu.PrefetchScalarGridSpec(
            num_scalar_prefetch=2,
            grid=(nj, p // tm),
            in_specs=[pl.BlockSpec((tm, k), lambda j, i, te, nt: (i, 0)),
                      pl.BlockSpec((1, k, tn), lambda j, i, te, nt: (te[i], 0, j)),
                      pl.BlockSpec((1, k, tn), lambda j, i, te, nt: (te[i], 0, j + nj))],
            out_specs=pl.BlockSpec((tm, tn), lambda j, i, te, nt: (i, j)),
            scratch_shapes=[pltpu.VMEM((k, tn), BF16), pltpu.VMEM((k, tn), BF16)]),
        compiler_params=_params("parallel", "arbitrary"),
        name="moe_expert_up",
    )(tile_expert, n_tiles_used, xs, w_gu, w_gu)


def _expert_down_kernel(te_ref, nt_ref, a_ref, w_ref, o_ref, wb_ref):
    i = pl.program_id(1)
    new_expert = jnp.logical_or(i == 0, te_ref[i] != te_ref[jnp.maximum(i - 1, 0)])

    @pl.when(new_expert)
    def _():
        wb_ref[...] = w_ref[0].astype(BF16)

    @pl.when(i < nt_ref[0])
    def _():
        o_ref[...] = _dot(a_ref[...], wb_ref[...]).astype(o_ref.dtype)

    @pl.when(i >= nt_ref[0])
    def _():
        o_ref[...] = jnp.zeros_like(o_ref)


def _expert_down(act, tile_expert, n_tiles_used, w_down, *, tn):
    p, k = act.shape
    n = w_down.shape[2]
    tm = MOE_TILE
    return pl.pallas_call(
        _expert_down_kernel,
        out_shape=jax.ShapeDtypeStruct((p, n), F32),
        grid_spec=pltpu.PrefetchScalarGridSpec(
            num_scalar_prefetch=2,
            grid=(n // tn, p // tm),
            in_specs=[pl.BlockSpec((tm, k), lambda j, i, te, nt: (i, 0)),
                      pl.BlockSpec((1, k, tn), lambda j, i, te, nt: (te[i], 0, j),
                                   pipeline_mode=pl.Buffered(1))],
            out_specs=pl.BlockSpec((tm, tn), lambda j, i, te, nt: (i, j)),
            scratch_shapes=[pltpu.VMEM((k, tn), BF16)]),
        compiler_params=_params("parallel", "arbitrary"),
        name="moe_expert_down",
    )(tile_expert, n_tiles_used, act, w_down)


def _combine_ln_kernel(slot_ref, ys_ref, w_ref, x_ref, mg_ref, lnw_ref, lnb_ref, o_ref,
                       buf_ref, sem, *, alpha, gate_idx):
    n = x_ref.shape[0]
    n_tok = n * pl.num_programs(0)
    i = pl.program_id(0)
    slot = i % 2

    def issue_tile(tile, dst_slot):
        def body(r, carry):
            for choice in range(2):
                _row_copy(ys_ref, buf_ref.at[dst_slot, choice], sem.at[dst_slot],
                          slot_ref[choice * n_tok + tile * n + r], r).start()
            return carry
        lax.fori_loop(0, n, body, 0, unroll=4)

    @pl.when(i == 0)
    def _():
        issue_tile(0, 0)

    @pl.when(i + 1 < pl.num_programs(0))
    def _():
        issue_tile(i + 1, 1 - slot)

    def wait_row(r, carry):
        for choice in range(2):
            _row_copy(ys_ref, buf_ref.at[slot, choice], sem.at[slot], 0, r).wait()
        return carry

    lax.fori_loop(0, n, wait_row, 0, unroll=4)
    w = w_ref[...]
    f = w[:, 0:1] * buf_ref[slot, 0] + w[:, 1:2] * buf_ref[slot, 1]
    gate = mg_ref[0][gate_idx:gate_idx + 1]
    t = alpha * x_ref[...] + gate * f
    o_ref[...] = _layer_norm_rows(t, lnw_ref[...], lnb_ref[...])


def _combine_ln(ys, slots, weights, x, mod_gate, gate_idx, ln_w, ln_b, alpha, tiles_per_batch):
    rows, d = x.shape
    tc = ROW_TILE
    row_spec = pl.BlockSpec((tc, d), lambda r, s: (r, 0))
    vec_spec = pl.BlockSpec((1, d), lambda r, s: (0, 0))
    return pl.pallas_call(
        functools.partial(_combine_ln_kernel, alpha=alpha, gate_idx=gate_idx),
        out_shape=jax.ShapeDtypeStruct((rows, d), F32),
        grid_spec=pltpu.PrefetchScalarGridSpec(
            num_scalar_prefetch=1,
            grid=(rows // tc,),
            in_specs=[pl.BlockSpec(memory_space=pl.ANY),
                      pl.BlockSpec((tc, 2), lambda r, s: (r, 0)),
                      row_spec,
                      pl.BlockSpec((1, 6, d), lambda r, s: (r // tiles_per_batch, 0, 0)),
                      vec_spec, vec_spec],
            out_specs=row_spec,
            scratch_shapes=[pltpu.VMEM((2, 2, tc, d), F32), pltpu.SemaphoreType.DMA((2,))]),
        compiler_params=_params("arbitrary"),
        name="moe_combine_layer_norm",
    )(slots, ys, weights, x, mod_gate, ln_w.reshape(1, d), ln_b.reshape(1, d))


def _routing_tables(top_i, n_slots):
    n_tok = top_i.shape[1]
    tm = MOE_TILE
    pair_e = top_i.T.reshape(-1)
    onehot = (pair_e[:, None] == jnp.arange(N_EXPERTS, dtype=jnp.int32)[None, :]).astype(jnp.int32)
    rank = jnp.cumsum(onehot, axis=0) - onehot
    counts = jnp.sum(onehot, axis=0)
    padded = ((counts + tm - 1) // tm) * tm
    ends = jnp.cumsum(padded)
    starts = ends - padded
    slot = jnp.sum(onehot * (starts[None, :] + rank), axis=1)
    token_of_slot = jnp.zeros((n_slots,), jnp.int32).at[slot].set(
        jnp.arange(2 * n_tok, dtype=jnp.int32) // 2)
    tile_start = jnp.arange(n_slots // tm, dtype=jnp.int32) * tm
    tile_expert = jnp.minimum(
        jnp.sum((tile_start[:, None] >= ends[None, :]).astype(jnp.int32), axis=1), N_EXPERTS - 1)
    n_tiles_used = (ends[-1:] // tm).astype(jnp.int32)
    slots = slot.reshape(n_tok, 2).T.reshape(-1)
    return slots, token_of_slot, tile_expert.astype(jnp.int32), n_tiles_used


def _moe(h_f32, router, w_gu, w_down):
    n_tok = h_f32.shape[0]
    top_i, top_w = _router(h_f32, router)
    n_slots = 2 * n_tok + N_EXPERTS * MOE_TILE
    slots, token_of_slot, tile_expert, n_tiles_used = _routing_tables(top_i, n_slots)
    xs = _gather_rows(h_f32, token_of_slot, BF16)
    act = _expert_up(xs, tile_expert, n_tiles_used, w_gu, tn=512)
    ys = _expert_down(act, tile_expert, n_tiles_used, w_down, tn=512)
    return ys, slots, top_w.T


def kernel(x, c, ctx, c_ctx, ada_w, ada_b, ln_w, ln_b, gdn_w_in, gdn_conv_w, gdn_a_log, gdn_dt_bias,
           gdn_norm_w, gdn_w_out, ffn_w_gu, ffn_w_down, diff_w_in, diff_lambda, diff_norm_w,
           diff_w_out, moe_router, moe_w_gu, moe_w_down):
    batch, n_lat, d = x.shape
    n_ctx = ctx.shape[1]
    depth = ada_w.shape[0]
    assert depth == 2 and n_ctx == ROW_TILE and n_lat % ROW_TILE == 0
    t_all = n_ctx + n_lat
    alpha = (2.0 * depth) ** 0.25
    tiles_all = t_all // ROW_TILE
    tiles_lat = n_lat // ROW_TILE

    n_cond = batch + 1
    cond = jnp.concatenate([c, c_ctx[None, :], jnp.zeros((-n_cond % 8, d), F32)], axis=0)
    mod = _ada_modulation(cond, ada_w, ada_b)[:, :n_cond].reshape(depth, n_cond, 6, d)

    xa = jnp.concatenate([ctx, x], axis=1).reshape(batch * t_all, d)

    h = _modulate(xa, mod[0], tiles_all, 1)
    y = _gdn_mixer(h, batch, gdn_w_in[0], gdn_conv_w[0], gdn_a_log[0], gdn_dt_bias[0],
                   gdn_norm_w[0], gdn_w_out[0])
    xa, h = _res_ln(xa, y, mod[0], 2, mod[0], 3, ln_w[0, 0], ln_b[0, 0], alpha, tiles_all, 1, BF16)
    act = _swiglu_up(h, ffn_w_gu[0], tm=1024, tn=512)
    f = _matmul(act, ffn_w_down[0], tm=512, tn=512, out_dtype=F32)
    xa, h = _res_ln(xa, f, mod[0], 5, mod[1], 0, ln_w[0, 1], ln_b[0, 1], alpha, tiles_all, 1, BF16)

    lam_init = 0.8 - 0.6 * math.exp(-0.3 * 1)
    cos, sin = _rope_tables(n_ctx, n_lat)
    qkv = _qkv_rope(h, diff_w_in[0], jnp.tile(cos, (batch, 1)), jnp.tile(sin, (batch, 1)),
                    tm=1024, tn=1024, q_scale=DIFF_DH ** -0.5 * math.log2(math.e))
    o = _diff_attention(qkv, batch, n_ctx, diff_lambda[0], diff_norm_w[0], lam_init)
    y = _matmul(o.reshape(batch * n_lat, DIFF_HEADS * DIFF_DV), diff_w_out[0],
                tm=1024, tn=1024, out_dtype=F32)
    xl = xa.reshape(batch, t_all, d)[:, n_ctx:].reshape(batch * n_lat, d)
    xl, hf = _res_ln(xl, y, mod[1], 2, mod[1], 3, ln_w[1, 0], ln_b[1, 0], alpha, tiles_lat, 0, F32)
    ys, slots, top_w = _moe(hf, moe_router[0], moe_w_gu[0], moe_w_down[0])
    out = _combine_ln(ys, slots, top_w, xl, mod[1], 5, ln_w[1, 1], ln_b[1, 1], alpha, tiles_lat)
    return out.reshape(batch, n_lat, d)
```

```python
import functools
import math

import jax
import jax.numpy as jnp
from jax import lax
from jax.experimental import pallas as pl
from jax.experimental.pallas import tpu as pltpu

F32 = jnp.float32
BF16 = jnp.bfloat16

GDN_QK_HEADS = 16
GDN_V_HEADS = 32
GDN_DK = 128
GDN_DV = 128
GDN_CONV = 5
GDN_CHUNK = 64
GDN_Q = GDN_QK_HEADS * GDN_DK
GDN_V = GDN_V_HEADS * GDN_DV
GDN_QKV = 2 * GDN_Q + GDN_V
DIFF_HEADS = 8
DIFF_DH = 128
DIFF_DV = 2 * DIFF_DH
DIFF_Q = DIFF_HEADS * 2 * DIFF_DH
GRID_W = 64
ROPE_BASE = 10000.0
ROPE_AXIS_DIM = DIFF_DH // 2
N_EXPERTS = 8
LN_EPS = 1e-5
RMS_EPS = 1e-6

V7X_VMEM_BYTES = 64 * 1024 * 1024
VMEM_LIMIT = V7X_VMEM_BYTES - 8 * 1024 * 1024
LANES = 128
ROW_TILE = 256
CHUNKS_PER_TILE = ROW_TILE // GDN_CHUNK
MOE_TILE = 512


def _params(*sem):
    return pltpu.CompilerParams(dimension_semantics=sem, vmem_limit_bytes=VMEM_LIMIT)


def _row_tile(m, preferred):
    t = preferred
    while m % t and t > ROW_TILE:
        t //= 2
    assert m % t == 0
    return t


def _dot(a, b):
    return jnp.dot(a, b, preferred_element_type=F32)


def _dot_nt(a, b):
    return lax.dot_general(a, b, (((1,), (1,)), ((), ())), preferred_element_type=F32)


def _dot_tn(a, b):
    return lax.dot_general(a, b, (((0,), (0,)), ((), ())), preferred_element_type=F32)


def _silu(x):
    return x * jax.nn.sigmoid(x)


def _ada_kernel(c_ref, w_ref, b_ref, o_ref):
    cond = _silu(c_ref[...])
    o_ref[0] = jnp.dot(cond, w_ref[0], preferred_element_type=F32,
                       precision=lax.Precision.HIGHEST) + b_ref[0]


def _ada_modulation(cond, ada_w, ada_b):
    depth, d, n = ada_w.shape
    rows = cond.shape[0]
    tn = 1024
    return pl.pallas_call(
        _ada_kernel,
        out_shape=jax.ShapeDtypeStruct((depth, rows, n), F32),
        grid=(depth, n // tn),
        in_specs=[pl.BlockSpec((rows, d), lambda l, j: (0, 0)),
                  pl.BlockSpec((1, d, tn), lambda l, j: (l, 0, j)),
                  pl.BlockSpec((1, 1, tn), lambda l, j: (l, 0, j))],
        out_specs=pl.BlockSpec((1, rows, tn), lambda l, j: (l, 0, j)),
        compiler_params=_params("parallel", "parallel"),
        name="ada_modulation",
    )(cond, ada_w, ada_b.reshape(depth, 1, n))


def _mod_index(r, tiles_per_batch, ctx_tiles, ctx_row):
    return jnp.where(r % tiles_per_batch < ctx_tiles, ctx_row, r // tiles_per_batch)


def _modulate_kernel(x_ref, m_ref, o_ref):
    m = m_ref[0]
    o_ref[...] = (x_ref[...] * (1.0 + m[1:2]) + m[0:1]).astype(o_ref.dtype)


def _modulate(x, mod, tiles_per_batch, ctx_tiles):
    rows, d = x.shape
    ctx_row = mod.shape[0] - 1
    midx = functools.partial(_mod_index, tiles_per_batch=tiles_per_batch,
                             ctx_tiles=ctx_tiles, ctx_row=ctx_row)
    return pl.pallas_call(
        _modulate_kernel,
        out_shape=jax.ShapeDtypeStruct((rows, d), BF16),
        grid=(rows // ROW_TILE,),
        in_specs=[pl.BlockSpec((ROW_TILE, d), lambda r: (r, 0)),
                  pl.BlockSpec((1, 6, d), lambda r: (midx(r), 0, 0))],
        out_specs=pl.BlockSpec((ROW_TILE, d), lambda r: (r, 0)),
        compiler_params=_params("parallel"),
        name="modulate",
    )(x, mod)


def _layer_norm_rows(t, w, b):
    mu = jnp.mean(t, axis=-1, keepdims=True)
    tc = t - mu
    var = jnp.mean(tc * tc, axis=-1, keepdims=True)
    return tc * lax.rsqrt(var + LN_EPS) * w + b


def _res_ln_kernel(x_ref, y_ref, mg_ref, mn_ref, lnw_ref, lnb_ref, xo_ref, ho_ref,
                   *, alpha, gate_idx, shift_idx):
    gate = mg_ref[0][gate_idx:gate_idx + 1]
    t = alpha * x_ref[...] + gate * y_ref[...].astype(F32)
    xn = _layer_norm_rows(t, lnw_ref[...], lnb_ref[...])
    xo_ref[...] = xn
    mn = mn_ref[0]
    h = xn * (1.0 + mn[shift_idx + 1:shift_idx + 2]) + mn[shift_idx:shift_idx + 1]
    ho_ref[...] = h.astype(ho_ref.dtype)


def _res_ln(x, y, mod_gate, gate_idx, mod_next, shift_idx, ln_w, ln_b, alpha,
            tiles_per_batch, ctx_tiles, h_dtype, x_skip_tiles=0):
    rows, d = y.shape
    ctx_row = mod_gate.shape[0] - 1
    x_tiles = tiles_per_batch + x_skip_tiles
    x_spec = pl.BlockSpec(
        (ROW_TILE, d), lambda r: ((r // tiles_per_batch) * x_tiles + x_skip_tiles + r % tiles_per_batch, 0))
    midx = functools.partial(_mod_index, tiles_per_batch=tiles_per_batch,
                             ctx_tiles=ctx_tiles, ctx_row=ctx_row)
    row_spec = pl.BlockSpec((ROW_TILE, d), lambda r: (r, 0))
    mod_spec = pl.BlockSpec((1, 6, d), lambda r: (midx(r), 0, 0))
    vec_spec = pl.BlockSpec((1, d), lambda r: (0, 0))
    return pl.pallas_call(
        functools.partial(_res_ln_kernel, alpha=alpha, gate_idx=gate_idx, shift_idx=shift_idx),
        out_shape=(jax.ShapeDtypeStruct((rows, d), F32), jax.ShapeDtypeStruct((rows, d), h_dtype)),
        grid=(rows // ROW_TILE,),
        in_specs=[x_spec, row_spec, mod_spec, mod_spec, vec_spec, vec_spec],
        out_specs=(row_spec, row_spec),
        compiler_params=_params("parallel"),
        name="residual_layer_norm",
    )(x, y, mod_gate, mod_next, ln_w.reshape(1, d), ln_b.reshape(1, d))


def _mm_kernel(a_ref, w_ref, o_ref, wbf_ref):
    @pl.when(pl.program_id(1) == 0)
    def _():
        wbf_ref[...] = w_ref[...].astype(BF16)

    o_ref[...] = _dot(a_ref[...], wbf_ref[...]).astype(o_ref.dtype)


def _matmul(a, w, *, tm, tn, out_dtype, col_start=0, n_cols=None):
    m, k = a.shape
    tm = _row_tile(m, tm)
    n_cols = w.shape[1] - col_start if n_cols is None else n_cols
    assert m % tm == 0 and n_cols % tn == 0 and col_start % tn == 0
    off = col_start // tn
    return pl.pallas_call(
        _mm_kernel,
        out_shape=jax.ShapeDtypeStruct((m, n_cols), out_dtype),
        grid=(n_cols // tn, m // tm),
        in_specs=[pl.BlockSpec((tm, k), lambda j, i: (i, 0)),
                  pl.BlockSpec((k, tn), lambda j, i: (0, j + off))],
        out_specs=pl.BlockSpec((tm, tn), lambda j, i: (i, j)),
        scratch_shapes=[pltpu.VMEM((k, tn), BF16)],
        compiler_params=_params("parallel", "arbitrary"),
        name="matmul",
    )(a, w)


def _swiglu_up_kernel(a_ref, wg_ref, wu_ref, o_ref, wgb_ref, wub_ref):
    @pl.when(pl.program_id(1) == 0)
    def _():
        wgb_ref[...] = wg_ref[...].astype(BF16)
        wub_ref[...] = wu_ref[...].astype(BF16)

    a = a_ref[...]
    g = _dot(a, wgb_ref[...])
    u = _dot(a, wub_ref[...])
    o_ref[...] = (_silu(g) * u).astype(o_ref.dtype)


def _swiglu_up(a, w_gu, *, tm, tn):
    m, k = a.shape
    tm = _row_tile(m, tm)
    ff = w_gu.shape[1] // 2
    assert m % tm == 0 and ff % tn == 0
    nj = ff // tn
    return pl.pallas_call(
        _swiglu_up_kernel,
        out_shape=jax.ShapeDtypeStruct((m, ff), BF16),
        grid=(nj, m // tm),
        in_specs=[pl.BlockSpec((tm, k), lambda j, i: (i, 0)),
                  pl.BlockSpec((k, tn), lambda j, i: (0, j)),
                  pl.BlockSpec((k, tn), lambda j, i: (0, j + nj))],
        out_specs=pl.BlockSpec((tm, tn), lambda j, i: (i, j)),
        scratch_shapes=[pltpu.VMEM((k, tn), BF16), pltpu.VMEM((k, tn), BF16)],
        compiler_params=_params("parallel", "arbitrary"),
        name="swiglu_up",
    )(a, w_gu, w_gu)


def _gdn_conv_kernel(cur_ref, prev_ref, next_ref, w_ref, o_ref, xe_ref, *, n_qk_blocks, q_scale):
    i = pl.program_id(1)
    j = pl.program_id(2)
    n_tiles = pl.num_programs(1)
    prev_ok = (i >= 2).astype(F32)
    next_ok = jnp.logical_and(i >= 1, i < n_tiles - 1).astype(F32)
    xe_ref[0:8, :] = prev_ref[0].astype(F32)[8:16] * prev_ok
    xe_ref[8:8 + ROW_TILE, :] = cur_ref[0].astype(F32)
    xe_ref[8 + ROW_TILE:16 + ROW_TILE, :] = next_ref[0].astype(F32)[0:8] * next_ok
    pad = (GDN_CONV - 1) // 2
    is_qk = j < n_qk_blocks
    scale = jnp.where(j < n_qk_blocks // 2, q_scale, 1.0)
    for g in range(xe_ref.shape[1] // GDN_DK):
        sl = slice(g * GDN_DK, (g + 1) * GDN_DK)
        xe = xe_ref[:, sl]
        acc = None
        for t in range(GDN_CONV):
            shifted = xe if t == pad else pltpu.roll(xe, (pad - t) % xe.shape[0], axis=0)
            term = shifted[8:8 + ROW_TILE] * w_ref[t:t + 1, sl]
            acc = term if acc is None else acc + term
        y = _silu(acc)
        inv = lax.rsqrt(jnp.sum(y * y, axis=-1, keepdims=True) + 1e-6) * scale
        o_ref[0, :, sl] = (y * jnp.where(is_qk, inv, 1.0)).astype(o_ref.dtype)


def _gdn_conv(pqz, conv_w, batch):
    rows, _ = pqz.shape
    t_all = rows // batch
    x = pqz.reshape(batch, t_all, pqz.shape[1])
    cb = 1024
    n_tiles = t_all // ROW_TILE
    halo = 16
    last_halo = t_all // halo - 1
    per_tile = ROW_TILE // halo
    return pl.pallas_call(
        functools.partial(_gdn_conv_kernel, n_qk_blocks=2 * GDN_Q // cb, q_scale=GDN_DK ** -0.5),
        out_shape=jax.ShapeDtypeStruct((batch, t_all, GDN_QKV), BF16),
        grid=(batch, n_tiles, GDN_QKV // cb),
        in_specs=[pl.BlockSpec((1, ROW_TILE, cb), lambda b, i, j: (b, i, j)),
                  pl.BlockSpec((1, halo, cb), lambda b, i, j: (b, jnp.maximum(i * per_tile - 1, 0), j)),
                  pl.BlockSpec((1, halo, cb),
                               lambda b, i, j: (b, jnp.minimum((i + 1) * per_tile, last_halo), j)),
                  pl.BlockSpec((GDN_CONV, cb), lambda b, i, j: (0, j))],
        out_specs=pl.BlockSpec((1, ROW_TILE, cb), lambda b, i, j: (b, i, j)),
        scratch_shapes=[pltpu.VMEM((ROW_TILE + 16, cb), F32)],
        compiler_params=_params("parallel", "parallel", "parallel"),
        name="gdn_conv",
    )(x, x, x, conv_w)


def _gdn_gate_kernel(ab_ref, alog_ref, dtb_ref, o_ref):
    ab = ab_ref[...]
    half = LANES // 2
    quarter = LANES // 4
    lane = lax.broadcasted_iota(jnp.int32, ab.shape, 1)
    beta = jax.nn.sigmoid(ab)
    x = ab + dtb_ref[...]
    softplus = jnp.maximum(x, 0.0) + jnp.log1p(jnp.exp(-jnp.abs(x)))
    g = jnp.where(lane >= half, -jnp.exp(alog_ref[...]) * softplus, 0.0)
    g = pltpu.roll(g, half, axis=1)
    n = ab.shape[0]
    ri = lax.broadcasted_iota(jnp.int32, (n, n), 0)
    ci = lax.broadcasted_iota(jnp.int32, (n, n), 1)
    same = (ri // GDN_CHUNK) == (ci // GDN_CHUNK)
    hi = lax.Precision.HIGHEST
    m_fwd = jnp.logical_and(same, ci <= ri).astype(F32)
    m_bwd = jnp.logical_and(same, ci >= ri).astype(F32)
    gc_f = jnp.dot(m_fwd, g, preferred_element_type=F32, precision=hi)
    gc_b = jnp.dot(m_bwd, g, preferred_element_type=F32, precision=hi)
    gc = jnp.where(lane < quarter, gc_f, gc_b)
    for d in range(2):
        parts = [pltpu.roll(val, (q * quarter - d * quarter) % LANES, axis=1)
                 for q, val in enumerate((beta, gc))]
        o_ref[d] = jnp.where(lane < quarter, parts[0], jnp.where(lane < 2 * quarter, parts[1], 0.0))


def _gdn_gates(ab, a_log, dt_bias):
    rows = ab.shape[0]
    zeros = jnp.zeros((LANES // 2,), F32)
    alog = jnp.concatenate([zeros, a_log.reshape(-1).astype(F32)]).reshape(1, LANES)
    dtb = jnp.concatenate([zeros, dt_bias.reshape(-1).astype(F32)]).reshape(1, LANES)
    return pl.pallas_call(
        _gdn_gate_kernel,
        out_shape=jax.ShapeDtypeStruct((2, rows, LANES), F32),
        grid=(rows // ROW_TILE,),
        in_specs=[pl.BlockSpec((ROW_TILE, LANES), lambda r: (r, 0)),
                  pl.BlockSpec((1, LANES), lambda r: (0, 0)),
                  pl.BlockSpec((1, LANES), lambda r: (0, 0))],
        out_specs=pl.BlockSpec((2, ROW_TILE, LANES), lambda r: (0, r, 0)),
        compiler_params=_params("parallel"),
        name="gdn_gates",
    )(ab, alog, dtb)


def _block_masks(ri, ci, n):
    masks = [(ri >> 1) == (ci >> 1)]
    for level in range(1, int(math.log2(n))):
        masks.append(jnp.logical_and((ri >> (level + 1)) == (ci >> (level + 1)),
                                     (ri >> level) != (ci >> level)))
    return masks


def _bmm(a, b):
    return jnp.einsum('hij,hjk->hik', a, b, preferred_element_type=F32)


def _bmm_nt(a, b):
    return jnp.einsum('hik,hjk->hij', a, b, preferred_element_type=F32)


def _bmm_tn(a, b):
    return jnp.einsum('hki,hkj->hij', a, b, preferred_element_type=F32)


def _pair_block_diag(x, left):
    return jnp.concatenate([jnp.where(left, x, 0.0), jnp.where(left, 0.0, x)], axis=1).astype(BF16)


def _unit_triangular_inverse(a, eye, masks, left):
    d = eye - jnp.where(masks[0], a, 0.0)
    for mask in masks[1:]:
        ad = _bmm(jnp.where(mask, a, 0.0).astype(BF16), _pair_block_diag(d, left))
        d = d - _bmm(d.astype(BF16), _pair_block_diag(ad, left))
    return d


def _gdn_scan_kernel(q_ref, k_ref, v_ref, g_ref, gr_ref, o_ref, s_ref,
                     a_sc, qk_sc, rhs_sc, qg_sc, kt_sc, egl_sc):
    d = pl.program_id(1)
    s = pl.program_id(2)

    @pl.when(s == 0)
    def _():
        s_ref[...] = jnp.zeros_like(s_ref)

    c = GDN_CHUNK
    n_h = GDN_V_HEADS
    rep = GDN_V_HEADS // GDN_QK_HEADS
    assert rep == 2
    rev = d == 1
    ri = lax.broadcasted_iota(jnp.int32, (c, 2 * c), 0)
    lane = lax.broadcasted_iota(jnp.int32, (c, 2 * c), 1)
    left = lane < c
    ci = lane % c
    later = jnp.where(rev, ci, ri)
    earlier = jnp.where(rev, ri, ci)
    incl = later >= earlier
    strict = later > earlier
    eye = (ri == ci).astype(F32)
    masks = _block_masks(ri, ci, c)

    def split_pairs(x):
        return jnp.stack([x[j, i * c:(i + 1) * c] for j in range(GDN_QK_HEADS) for i in range(rep)])

    def prepare(cc):
        off = pl.multiple_of(cc * c, c)
        q_all = q_ref[0, pl.ds(off, c), :]
        k_all = k_ref[0, pl.ds(off, c), :]
        v_all = v_ref[0, pl.ds(off, c), :]
        gates = g_ref[0, pl.ds(off, c), :]
        gc_rows = gr_ref[0, 0, cc]
        qs = [q_all[:, i * GDN_DK:(i + 1) * GDN_DK] for i in range(GDN_QK_HEADS)]
        ks = [k_all[:, i * GDN_DK:(i + 1) * GDN_DK] for i in range(GDN_QK_HEADS)]
        q_over_k = jnp.stack([jnp.concatenate([qs[i], ks[i]], axis=0) for i in range(GDN_QK_HEADS)])
        k_twice = jnp.stack([jnp.concatenate([ks[i], ks[i]], axis=0) for i in range(GDN_QK_HEADS)])
        qk_kk = _bmm_nt(q_over_k, k_twice)

        def col(quantity, h):
            idx = quantity * n_h + h
            return gates[:, idx:idx + 1]

        a_l, qk_l, rhs_l, qg_l, kt_l, egl_l = [], [], [], [], [], []
        for j in range(GDN_QK_HEADS):
            kf = ks[j].astype(F32)
            qf = qs[j].astype(F32)
            rhs_pair, beta_pair, gc_pair = [], [], []
            for h in (rep * j, rep * j + 1):
                beta = jnp.broadcast_to(col(0, h), (c, 2 * c))
                gc = jnp.broadcast_to(col(1, h), (c, 2 * c))
                g_total = jnp.where(rev, gc[0:1, :], gc[c - 1:c, :])
                eg = jnp.exp(gc)
                ek = jnp.exp(g_total - gc)
                vf = v_all[:, h * GDN_DV:(h + 1) * GDN_DV].astype(F32)
                rhs_pair.append(jnp.concatenate([vf * beta, kf * (beta * eg)], axis=1).astype(BF16))
                qg_l.append((qf * eg).astype(BF16))
                kt_l.append((kf * ek).T.astype(BF16))
                egl_l.append(jnp.exp(g_total))
                beta_pair.append(beta)
                gc_pair.append(gc)
            rhs_l.append(jnp.concatenate(rhs_pair, axis=0))
            beta_p = jnp.where(left, beta_pair[0], beta_pair[1])
            gc_p = jnp.where(left, gc_pair[0], gc_pair[1])
            gdiff = gc_p - gc_rows[j:j + 1, :]
            decay = jnp.where(incl, jnp.exp(jnp.where(incl, gdiff, 0.0)), 0.0)
            a_l.append(jnp.where(strict, qk_kk[j, c:] * (beta_p * decay), 0.0))
            qk_l.append(qk_kk[j, 0:c] * decay)
        return a_l, qk_l, rhs_l, qg_l, kt_l, egl_l

    staged = (a_sc, qk_sc, rhs_sc, qg_sc, kt_sc, egl_sc)

    def chunk_of(position):
        position = jnp.minimum(position, CHUNKS_PER_TILE - 1)
        return jnp.where(rev, CHUNKS_PER_TILE - 1 - position, position)

    def stage(position, slot):
        for ref, parts in zip(staged, prepare(chunk_of(position))):
            ref[slot] = jnp.stack(parts)

    def advance(position, slot):
        t_inv = _unit_triangular_inverse(a_sc[slot], eye, masks, left)
        uw = split_pairs(_bmm(_pair_block_diag(t_inv, left), rhs_sc[slot]))
        u = uw[:, :, 0:GDN_DV]
        w = uw[:, :, GDN_DV:]
        state = s_ref[...]
        state_b = state.astype(BF16)
        w_qg = _bmm(jnp.concatenate([w.astype(BF16), qg_sc[slot]], axis=1), state_b)
        v_new = u - w_qg[:, 0:c]
        v_new_b = v_new.astype(BF16)
        v_pairs = jnp.stack([jnp.concatenate([v_new_b[rep * j], v_new_b[rep * j + 1]], axis=0)
                             for j in range(GDN_QK_HEADS)])
        o = w_qg[:, c:] + split_pairs(_bmm(_pair_block_diag(qk_sc[slot], left), v_pairs))
        s_ref[...] = state * egl_sc[slot] + _bmm(kt_sc[slot], v_new_b)
        off = pl.multiple_of(chunk_of(position) * c, c)
        for h in range(n_h):
            o_ref[0, 0, pl.ds(off, c), h * GDN_DV:(h + 1) * GDN_DV] = o[h].astype(o_ref.dtype)

    def pair_step(step, carry):
        first = 2 * step
        stage(first, 0)
        advance(first, 0)
        stage(first + 1, 1)
        advance(first + 1, 1)
        return carry

    lax.fori_loop(0, CHUNKS_PER_TILE // 2, pair_step, 0)


def _gdn_scan(qkv, gates, gc_rows, batch):
    _, t_all, _ = qkv.shape
    n_tiles = t_all // ROW_TILE

    def tile(d, s):
        return jnp.where(d == 0, s, jnp.where(s == 0, 0, n_tiles - s))

    return pl.pallas_call(
        _gdn_scan_kernel,
        out_shape=jax.ShapeDtypeStruct((2, batch, t_all, GDN_V), BF16),
        grid=(batch, 2, n_tiles),
        in_specs=[
            pl.BlockSpec((1, ROW_TILE, GDN_Q), lambda b, d, s: (b, tile(d, s), 0)),
            pl.BlockSpec((1, ROW_TILE, GDN_Q), lambda b, d, s: (b, tile(d, s), 1)),
            pl.BlockSpec((1, ROW_TILE, GDN_V), lambda b, d, s: (b, tile(d, s), 1)),
            pl.BlockSpec((1, ROW_TILE, LANES), lambda b, d, s: (d, b * n_tiles + tile(d, s), 0)),
            pl.BlockSpec((1, 1, CHUNKS_PER_TILE, GDN_QK_HEADS, 2 * GDN_CHUNK),
                         lambda b, d, s: (b, d, tile(d, s), 0, 0)),
        ],
        out_specs=pl.BlockSpec((1, 1, ROW_TILE, GDN_V), lambda b, d, s: (d, b, tile(d, s), 0)),
        scratch_shapes=[
            pltpu.VMEM((GDN_V_HEADS, GDN_DK, GDN_DV), F32),
            pltpu.VMEM((2, GDN_QK_HEADS, GDN_CHUNK, 2 * GDN_CHUNK), F32),
            pltpu.VMEM((2, GDN_QK_HEADS, GDN_CHUNK, 2 * GDN_CHUNK), F32),
            pltpu.VMEM((2, GDN_QK_HEADS, 2 * GDN_CHUNK, GDN_DV + GDN_DK), BF16),
            pltpu.VMEM((2, GDN_V_HEADS, GDN_CHUNK, GDN_DK), BF16),
            pltpu.VMEM((2, GDN_V_HEADS, GDN_DK, GDN_CHUNK), BF16),
            pltpu.VMEM((2, GDN_V_HEADS, 1, GDN_DV), F32),
        ],
        compiler_params=_params("parallel", "arbitrary", "arbitrary"),
        name="gdn_scan",
    )(qkv, qkv, qkv, gates, gc_rows)


def _gdn_norm_gate_kernel(of_ref, ob_ref, z_ref, w_ref, o_ref):
    w = w_ref[...]
    for g in range(o_ref.shape[1] // GDN_DV):
        sl = slice(g * GDN_DV, (g + 1) * GDN_DV)
        og = of_ref[0, :, sl].astype(F32) + ob_ref[0, :, sl].astype(F32)
        inv = lax.rsqrt(jnp.mean(og * og, axis=-1, keepdims=True) + RMS_EPS)
        o_ref[:, sl] = (og * inv * w * _silu(z_ref[:, sl].astype(F32))).astype(o_ref.dtype)


def _gdn_norm_gate(o2, pqz, norm_w):
    _, rows, width = o2.shape
    cb = width
    z_off = GDN_QKV // cb
    return pl.pallas_call(
        _gdn_norm_gate_kernel,
        out_shape=jax.ShapeDtypeStruct((rows, width), BF16),
        grid=(rows // ROW_TILE, width // cb),
        in_specs=[pl.BlockSpec((1, ROW_TILE, cb), lambda r, j: (0, r, j)),
                  pl.BlockSpec((1, ROW_TILE, cb), lambda r, j: (1, r, j)),
                  pl.BlockSpec((ROW_TILE, cb), lambda r, j: (r, z_off + j)),
                  pl.BlockSpec((1, GDN_DV), lambda r, j: (0, 0))],
        out_specs=pl.BlockSpec((ROW_TILE, cb), lambda r, j: (r, j)),
        compiler_params=_params("parallel", "parallel"),
        name="gdn_norm_gate",
    )(o2, o2, pqz, norm_w.reshape(1, GDN_DV))


def _gdn_mixer(h, batch, w_in, conv_w, a_log, dt_bias, norm_w, w_out):
    rows = h.shape[0]
    t_all = rows // batch
    pqz = _matmul(h, w_in, tm=1024, tn=1024, out_dtype=BF16, col_start=0, n_cols=GDN_QKV + GDN_V)
    ab = _matmul(h, w_in, tm=1024, tn=LANES, out_dtype=F32, col_start=GDN_QKV + GDN_V, n_cols=LANES)
    gates = _gdn_gates(ab, a_log, dt_bias)
    n_chunks = t_all // GDN_CHUNK
    gc_rows = gates[:, :, GDN_V_HEADS:2 * GDN_V_HEADS]
    gc_rows = gc_rows.reshape(2, batch, n_chunks, GDN_CHUNK, GDN_V_HEADS).transpose(1, 0, 2, 4, 3)
    gc_rows = gc_rows.reshape(batch, 2, n_chunks, GDN_QK_HEADS, 2 * GDN_CHUNK)
    qkv = _gdn_conv(pqz, conv_w, batch)
    o2 = _gdn_scan(qkv, gates, gc_rows, batch)
    og = _gdn_norm_gate(o2.reshape(2, rows, GDN_V), pqz, norm_w)
    return _matmul(og, w_out, tm=1024, tn=512, out_dtype=F32)


def _rope_tables(n_ctx, n_lat):
    rows = n_lat // GRID_W
    row = jnp.repeat(jnp.arange(rows, dtype=F32), GRID_W)
    col = jnp.tile(jnp.arange(GRID_W, dtype=F32), rows)
    inv_freq = 1.0 / (ROPE_BASE ** (jnp.arange(0, ROPE_AXIS_DIM, 2, dtype=F32) / ROPE_AXIS_DIM))
    ang_r = row[:, None] * inv_freq
    ang_c = col[:, None] * inv_freq
    cos = jnp.concatenate([jnp.cos(ang_r), jnp.cos(ang_r), jnp.cos(ang_c), jnp.cos(ang_c)], axis=1)
    sin = jnp.concatenate([-jnp.sin(ang_r), jnp.sin(ang_r), -jnp.sin(ang_c), jnp.sin(ang_c)], axis=1)
    cos = jnp.concatenate([jnp.ones((n_ctx, DIFF_DH), F32), cos], axis=0)
    sin = jnp.concatenate([jnp.zeros((n_ctx, DIFF_DH), F32), sin], axis=0)
    return cos, sin


def _qkv_rope_kernel(a_ref, w_ref, cos_ref, sin_ref, o_ref, wbf_ref, *, n_rope_blocks, n_q_blocks,
                     q_scale):
    j = pl.program_id(0)

    @pl.when(pl.program_id(1) == 0)
    def _():
        wbf_ref[...] = w_ref[...].astype(BF16)

    acc = _dot(a_ref[...], wbf_ref[...])

    @pl.when(j >= n_rope_blocks)
    def _():
        o_ref[...] = acc.astype(o_ref.dtype)

    @pl.when(j < n_rope_blocks)
    def _():
        scale = jnp.where(j < n_q_blocks, q_scale, 1.0)
        cos = cos_ref[...] * scale
        sin = sin_ref[...] * scale
        quarter = ROPE_AXIS_DIM // 2
        lane = lax.broadcasted_iota(jnp.int32, cos.shape, 1)
        first = (lane % ROPE_AXIS_DIM) < quarter
        for g in range(acc.shape[1] // DIFF_DH):
            sl = slice(g * DIFF_DH, (g + 1) * DIFF_DH)
            xg = acc[:, sl]
            partner = jnp.where(first, pltpu.roll(xg, DIFF_DH - quarter, axis=1),
                                pltpu.roll(xg, quarter, axis=1))
            o_ref[:, sl] = (xg * cos + partner * sin).astype(o_ref.dtype)


def _qkv_rope(a, w, cos, sin, *, tm, tn, q_scale):
    m, k = a.shape
    tm = _row_tile(m, tm)
    n = w.shape[1]
    assert n % tn == 0 and DIFF_Q % tn == 0
    return pl.pallas_call(
        functools.partial(_qkv_rope_kernel, n_rope_blocks=2 * DIFF_Q // tn, n_q_blocks=DIFF_Q // tn,
                          q_scale=q_scale),
        out_shape=jax.ShapeDtypeStruct((m, n), BF16),
        grid=(n // tn, m // tm),
        in_specs=[pl.BlockSpec((tm, k), lambda j, i: (i, 0)),
                  pl.BlockSpec((k, tn), lambda j, i: (0, j)),
                  pl.BlockSpec((tm, DIFF_DH), lambda j, i: (i, 0)),
                  pl.BlockSpec((tm, DIFF_DH), lambda j, i: (i, 0))],
        out_specs=pl.BlockSpec((tm, tn), lambda j, i: (i, j)),
        scratch_shapes=[pltpu.VMEM((k, tn), BF16)],
        compiler_params=_params("parallel", "arbitrary"),
        name="qkv_rope",
    )(a, w, cos, sin)


def _diff_attn_kernel(lam_ref, nw_ref, q_ref, k_ref, v_ref, o_ref, *, lam_init, key_chunk):
    lp = lam_ref[...]
    lam = (jnp.exp(jnp.sum(lp[0:1] * lp[1:2], axis=-1, keepdims=True))
           - jnp.exp(jnp.sum(lp[2:3] * lp[3:4], axis=-1, keepdims=True)) + lam_init)
    q = q_ref[0]
    n_keys = k_ref.shape[1]

    def attend(m):
        sl = slice(m * DIFF_DH, (m + 1) * DIFF_DH)
        qm = q[:, sl]
        m_run = l_run = acc = None
        for c in range(pl.cdiv(n_keys, key_chunk)):
            rows = slice(c * key_chunk, min((c + 1) * key_chunk, n_keys))
            s2 = _dot_nt(qm, k_ref[0, rows, sl])
            m_c = jnp.max(s2, axis=-1, keepdims=True)
            if c == 0:
                m_new = m_c
                p = jnp.exp2(s2 - m_new)
                l_run = jnp.sum(p, axis=-1, keepdims=True)
                acc = _dot(p.astype(BF16), v_ref[0, rows, :])
            else:
                m_new = jnp.maximum(m_run, m_c)
                rescale = jnp.exp2(m_run - m_new)
                p = jnp.exp2(s2 - m_new)
                l_run = l_run * rescale + jnp.sum(p, axis=-1, keepdims=True)
                acc = acc * rescale + _dot(p.astype(BF16), v_ref[0, rows, :])
            m_run = m_new
        return acc, l_run

    o1, l1 = attend(0)
    o2, l2 = attend(1)
    o = o1 * (1.0 / l1) - o2 * (lam / l2)
    inv = lax.rsqrt(jnp.mean(o * o, axis=-1, keepdims=True) + RMS_EPS)
    o_ref[0] = (o * inv * nw_ref[...] * (1.0 - lam_init)).astype(o_ref.dtype)


def _diff_attention(qkv, batch, n_ctx, lam_p, norm_w, lam_init):
    t_all = qkv.shape[0] // batch
    n_lat = t_all - n_ctx
    tq = ROW_TILE
    assert n_ctx % tq == 0
    ctx_tiles = n_ctx // tq
    v3 = qkv.reshape(batch, t_all, qkv.shape[1])
    k_off = DIFF_Q // DIFF_DV
    v_off = 2 * DIFF_Q // DIFF_DV
    return pl.pallas_call(
        functools.partial(_diff_attn_kernel, lam_init=lam_init, key_chunk=2 * ROW_TILE),
        out_shape=jax.ShapeDtypeStruct((batch, n_lat, DIFF_HEADS * DIFF_DV), BF16),
        grid=(batch, DIFF_HEADS, n_lat // tq),
        in_specs=[pl.BlockSpec((4, DIFF_DH), lambda b, h, i: (0, 0)),
                  pl.BlockSpec((1, DIFF_DV), lambda b, h, i: (0, 0)),
                  pl.BlockSpec((1, tq, DIFF_DV), lambda b, h, i: (b, i + ctx_tiles, h)),
                  pl.BlockSpec((1, t_all, DIFF_DV), lambda b, h, i: (b, 0, k_off + h)),
                  pl.BlockSpec((1, t_all, DIFF_DV), lambda b, h, i: (b, 0, v_off + h))],
        out_specs=pl.BlockSpec((1, tq, DIFF_DV), lambda b, h, i: (b, i, h)),
        compiler_params=_params("parallel", "parallel", "arbitrary"),
        name="diff_attention",
    )(lam_p.astype(F32), norm_w.reshape(1, DIFF_DV), v3, v3, v3)


def _router_kernel(h_ref, r_ref, idx_ref, w_ref):
    logits = _dot_nt(r_ref[...].astype(BF16), h_ref[...].astype(BF16))
    e = lax.broadcasted_iota(jnp.int32, logits.shape, 0)
    n_e = logits.shape[0]
    m1 = jnp.max(logits, axis=0, keepdims=True)
    i1 = jnp.min(jnp.where(logits == m1, e, n_e), axis=0, keepdims=True)
    rest = jnp.where(e == i1, -jnp.inf, logits)
    m2 = jnp.max(rest, axis=0, keepdims=True)
    i2 = jnp.min(jnp.where(rest == m2, e, n_e), axis=0, keepdims=True)
    t = jnp.exp(m2 - m1)
    idx_ref[0:1, :] = i1
    idx_ref[1:2, :] = i2
    w_ref[0:1, :] = 1.0 / (1.0 + t)
    w_ref[1:2, :] = t / (1.0 + t)


def _router(h, router):
    rows, d = h.shape
    tm = _row_tile(rows, 1024)
    return pl.pallas_call(
        _router_kernel,
        out_shape=(jax.ShapeDtypeStruct((2, rows), jnp.int32), jax.ShapeDtypeStruct((2, rows), F32)),
        grid=(rows // tm,),
        in_specs=[pl.BlockSpec((tm, d), lambda i: (i, 0)),
                  pl.BlockSpec((N_EXPERTS, d), lambda i: (0, 0))],
        out_specs=(pl.BlockSpec((2, tm), lambda i: (0, i)), pl.BlockSpec((2, tm), lambda i: (0, i))),
        compiler_params=_params("parallel"),
        name="moe_router",
    )(h, router.T)


def _row_copy(src_hbm, dst_vmem, sem, src_row, dst_row):
    return pltpu.make_async_copy(src_hbm.at[pl.ds(src_row, 1)], dst_vmem.at[pl.ds(dst_row, 1)], sem)


def _gather_rows_kernel(idx_ref, src_ref, o_ref, buf_ref, sem):
    n = o_ref.shape[0]
    i = pl.program_id(0)
    slot = i % 2

    def issue_tile(tile, dst_slot):
        def body(r, carry):
            _row_copy(src_ref, buf_ref.at[dst_slot], sem.at[dst_slot], idx_ref[tile * n + r], r).start()
            return carry
        lax.fori_loop(0, n, body, 0, unroll=8)

    @pl.when(i == 0)
    def _():
        issue_tile(0, 0)

    @pl.when(i + 1 < pl.num_programs(0))
    def _():
        issue_tile(i + 1, 1 - slot)

    def wait_row(r, carry):
        _row_copy(src_ref, buf_ref.at[slot], sem.at[slot], 0, r).wait()
        return carry

    lax.fori_loop(0, n, wait_row, 0, unroll=8)
    o_ref[...] = buf_ref[slot].astype(o_ref.dtype)


def _gather_rows(src, idx, out_dtype):
    n = idx.shape[0]
    d = src.shape[1]
    tg = ROW_TILE
    return pl.pallas_call(
        _gather_rows_kernel,
        out_shape=jax.ShapeDtypeStruct((n, d), out_dtype),
        grid_spec=pltpu.PrefetchScalarGridSpec(
            num_scalar_prefetch=1,
            grid=(n // tg,),
            in_specs=[pl.BlockSpec(memory_space=pl.ANY)],
            out_specs=pl.BlockSpec((tg, d), lambda i, idx: (i, 0)),
            scratch_shapes=[pltpu.VMEM((2, tg, d), src.dtype), pltpu.SemaphoreType.DMA((2,))]),
        compiler_params=_params("arbitrary"),
        name="moe_gather",
    )(idx, src)


def _expert_up_kernel(te_ref, nt_ref, a_ref, wg_ref, wu_ref, o_ref, wgb_ref, wub_ref):
    i = pl.program_id(1)
    new_expert = jnp.logical_or(i == 0, te_ref[i] != te_ref[jnp.maximum(i - 1, 0)])

    @pl.when(new_expert)
    def _():
        wgb_ref[...] = wg_ref[0].astype(BF16)
        wub_ref[...] = wu_ref[0].astype(BF16)

    @pl.when(i < nt_ref[0])
    def _():
        a = a_ref[...]
        g = _dot(a, wgb_ref[...])
        u = _dot(a, wub_ref[...])
        o_ref[...] = (_silu(g) * u).astype(o_ref.dtype)

    @pl.when(i >= nt_ref[0])
    def _():
        o_ref[...] = jnp.zeros_like(o_ref)


def _expert_up(xs, tile_expert, n_tiles_used, w_gu, *, tn):
    p, k = xs.shape
    ff = w_gu.shape[2] // 2
    nj = ff // tn
    tm = MOE_TILE
    return pl.pallas_call(
        _expert_up_kernel,
        out_shape=jax.ShapeDtypeStruct((p, ff), BF16),
        grid_spec=pltpu.PrefetchScalarGridSpec(
            num_scalar_prefetch=2,
            grid=(nj, p // tm),
            in_specs=[pl.BlockSpec((tm, k), lambda j, i, te, nt: (i, 0)),
                      pl.BlockSpec((1, k, tn), lambda j, i, te, nt: (te[i], 0, j)),
                      pl.BlockSpec((1, k, tn), lambda j, i, te, nt: (te[i], 0, j + nj))],
            out_specs=pl.BlockSpec((tm, tn), lambda j, i, te, nt: (i, j)),
            scratch_shapes=[pltpu.VMEM((k, tn), BF16), pltpu.VMEM((k, tn), BF16)]),
        compiler_params=_params("parallel", "arbitrary"),
        name="moe_expert_up",
    )(tile_expert, n_tiles_used, xs, w_gu, w_gu)


def _expert_down_kernel(te_ref, nt_ref, a_ref, w_ref, o_ref, wb_ref):
    i = pl.program_id(1)
    new_expert = jnp.logical_or(i == 0, te_ref[i] != te_ref[jnp.maximum(i - 1, 0)])

    @pl.when(new_expert)
    def _():
        wb_ref[...] = w_ref[0].astype(BF16)

    @pl.when(i < nt_ref[0])
    def _():
        o_ref[...] = _dot(a_ref[...], wb_ref[...]).astype(o_ref.dtype)

    @pl.when(i >= nt_ref[0])
    def _():
        o_ref[...] = jnp.zeros_like(o_ref)


def _expert_down(act, tile_expert, n_tiles_used, w_down, *, tn):
    p, k = act.shape
    n = w_down.shape[2]
    tm = MOE_TILE
    return pl.pallas_call(
        _expert_down_kernel,
        out_shape=jax.ShapeDtypeStruct((p, n), F32),
        grid_spec=pltpu.PrefetchScalarGridSpec(
            num_scalar_prefetch=2,
            grid=(n // tn, p // tm),
            in_specs=[pl.BlockSpec((tm, k), lambda j, i, te, nt: (i, 0)),
                      pl.BlockSpec((1, k, tn), lambda j, i, te, nt: (te[i], 0, j),
                                   pipeline_mode=pl.Buffered(1))],
            out_specs=pl.BlockSpec((tm, tn), lambda j, i, te, nt: (i, j)),
            scratch_shapes=[pltpu.VMEM((k, tn), BF16)]),
        compiler_params=_params("parallel", "arbitrary"),
        name="moe_expert_down",
    )(tile_expert, n_tiles_used, act, w_down)


def _combine_ln_kernel(slot_ref, ys_ref, w_ref, x_ref, mg_ref, lnw_ref, lnb_ref, o_ref,
                       buf_ref, sem, *, alpha, gate_idx):
    n = x_ref.shape[0]
    n_tok = n * pl.num_programs(0)
    i = pl.program_id(0)
    slot = i % 2

    def issue_tile(tile, dst_slot):
        def body(r, carry):
            for choice in range(2):
                _row_copy(ys_ref, buf_ref.at[dst_slot, choice], sem.at[dst_slot],
                          slot_ref[choice * n_tok + tile * n + r], r).start()
            return carry
        lax.fori_loop(0, n, body, 0, unroll=4)

    @pl.when(i == 0)
    def _():
        issue_tile(0, 0)

    @pl.when(i + 1 < pl.num_programs(0))
    def _():
        issue_tile(i + 1, 1 - slot)

    def wait_row(r, carry):
        for choice in range(2):
            _row_copy(ys_ref, buf_ref.at[slot, choice], sem.at[slot], 0, r).wait()
        return carry

    lax.fori_loop(0, n, wait_row, 0, unroll=4)
    w = w_ref[...]
    f = w[:, 0:1] * buf_ref[slot, 0] + w[:, 1:2] * buf_ref[slot, 1]
    gate = mg_ref[0][gate_idx:gate_idx + 1]
    t = alpha * x_ref[...] + gate * f
    o_ref[...] = _layer_norm_rows(t, lnw_ref[...], lnb_ref[...])


def _combine_ln(ys, slots, weights, x, mod_gate, gate_idx, ln_w, ln_b, alpha, tiles_per_batch):
    rows, d = x.shape
    tc = ROW_TILE
    row_spec = pl.BlockSpec((tc, d), lambda r, s: (r, 0))
    vec_spec = pl.BlockSpec((1, d), lambda r, s: (0, 0))
    return pl.pallas_call(
        functools.partial(_combine_ln_kernel, alpha=alpha, gate_idx=gate_idx),
        out_shape=jax.ShapeDtypeStruct((rows, d), F32),
        grid_spec=pltpu.PrefetchScalarGridSpec(
            num_scalar_prefetch=1,
            grid=(rows // tc,),
            in_specs=[pl.BlockSpec(memory_space=pl.ANY),
                      pl.BlockSpec((tc, 2), lambda r, s: (r, 0)),
                      row_spec,
                      pl.BlockSpec((1, 6, d), lambda r, s: (r // tiles_per_batch, 0, 0)),
                      vec_spec, vec_spec],
            out_specs=row_spec,
            scratch_shapes=[pltpu.VMEM((2, 2, tc, d), F32), pltpu.SemaphoreType.DMA((2,))]),
        compiler_params=_params("arbitrary"),
        name="moe_combine_layer_norm",
    )(slots, ys, weights, x, mod_gate, ln_w.reshape(1, d), ln_b.reshape(1, d))


def _routing_tables(top_i, n_slots):
    n_tok = top_i.shape[1]
    tm = MOE_TILE
    pair_e = top_i.T.reshape(-1)
    onehot = (pair_e[:, None] == jnp.arange(N_EXPERTS, dtype=jnp.int32)[None, :]).astype(jnp.int32)
    rank = jnp.cumsum(onehot, axis=0) - onehot
    counts = jnp.sum(onehot, axis=0)
    padded = ((counts + tm - 1) // tm) * tm
    ends = jnp.cumsum(padded)
    starts = ends - padded
    slot = jnp.sum(onehot * (starts[None, :] + rank), axis=1)
    token_of_slot = jnp.zeros((n_slots,), jnp.int32).at[slot].set(
        jnp.arange(2 * n_tok, dtype=jnp.int32) // 2)
    tile_start = jnp.arange(n_slots // tm, dtype=jnp.int32) * tm
    tile_expert = jnp.minimum(
        jnp.sum((tile_start[:, None] >= ends[None, :]).astype(jnp.int32), axis=1), N_EXPERTS - 1)
    n_tiles_used = (ends[-1:] // tm).astype(jnp.int32)
    slots = slot.reshape(n_tok, 2).T.reshape(-1)
    return slots, token_of_slot, tile_expert.astype(jnp.int32), n_tiles_used


def _moe(h_f32, router, w_gu, w_down):
    n_tok = h_f32.shape[0]
    top_i, top_w = _router(h_f32, router)
    n_slots = 2 * n_tok + N_EXPERTS * MOE_TILE
    slots, token_of_slot, tile_expert, n_tiles_used = _routing_tables(top_i, n_slots)
    xs = _gather_rows(h_f32, token_of_slot, BF16)
    act = _expert_up(xs, tile_expert, n_tiles_used, w_gu, tn=1024)
    ys = _expert_down(act, tile_expert, n_tiles_used, w_down, tn=512)
    return ys, slots, top_w.T


def kernel(x, c, ctx, c_ctx, ada_w, ada_b, ln_w, ln_b, gdn_w_in, gdn_conv_w, gdn_a_log, gdn_dt_bias,
           gdn_norm_w, gdn_w_out, ffn_w_gu, ffn_w_down, diff_w_in, diff_lambda, diff_norm_w,
           diff_w_out, moe_router, moe_w_gu, moe_w_down):
    batch, n_lat, d = x.shape
    n_ctx = ctx.shape[1]
    depth = ada_w.shape[0]
    assert depth == 2 and n_ctx == ROW_TILE and n_lat % ROW_TILE == 0
    t_all = n_ctx + n_lat
    alpha = (2.0 * depth) ** 0.25
    tiles_all = t_all // ROW_TILE
    tiles_lat = n_lat // ROW_TILE

    n_cond = batch + 1
    cond = jnp.concatenate([c, c_ctx[None, :], jnp.zeros((-n_cond % 8, d), F32)], axis=0)
    mod = _ada_modulation(cond, ada_w, ada_b)[:, :n_cond].reshape(depth, n_cond, 6, d)

    xa = jnp.concatenate([ctx, x], axis=1).reshape(batch * t_all, d)

    h = _modulate(xa, mod[0], tiles_all, 1)
    y = _gdn_mixer(h, batch, gdn_w_in[0], gdn_conv_w[0], gdn_a_log[0], gdn_dt_bias[0],
                   gdn_norm_w[0], gdn_w_out[0])
    xa, h = _res_ln(xa, y, mod[0], 2, mod[0], 3, ln_w[0, 0], ln_b[0, 0], alpha, tiles_all, 1, BF16)
    act = _swiglu_up(h, ffn_w_gu[0], tm=1024, tn=512)
    f = _matmul(act, ffn_w_down[0], tm=512, tn=512, out_dtype=F32)
    xa, h = _res_ln(xa, f, mod[0], 5, mod[1], 0, ln_w[0, 1], ln_b[0, 1], alpha, tiles_all, 1, BF16)

    lam_init = 0.8 - 0.6 * math.exp(-0.3 * 1)
    cos, sin = _rope_tables(n_ctx, n_lat)
    qkv = _qkv_rope(h, diff_w_in[0], jnp.tile(cos, (batch, 1)), jnp.tile(sin, (batch, 1)),
                    tm=1024, tn=1024, q_scale=DIFF_DH ** -0.5 * math.log2(math.e))
    o = _diff_attention(qkv, batch, n_ctx, diff_lambda[0], diff_norm_w[0], lam_init)
    y = _matmul(o.reshape(batch * n_lat, DIFF_HEADS * DIFF_DV), diff_w_out[0],
                tm=1024, tn=1024, out_dtype=F32)
    xl, hf = _res_ln(xa, y, mod[1], 2, mod[1], 3, ln_w[1, 0], ln_b[1, 0], alpha, tiles_lat, 0, F32,
                     x_skip_tiles=tiles_all - tiles_lat)
    ys, slots, top_w = _moe(hf, moe_router[0], moe_w_gu[0], moe_w_down[0])
    out = _combine_ln(ys, slots, top_w, xl, mod[1], 5, ln_w[1, 1], ln_b[1, 1], alpha, tiles_lat)
    return out.reshape(batch, n_lat, d)
```

```python
import functools
import math

import jax
import jax.numpy as jnp
from jax import lax
from jax.experimental import pallas as pl
from jax.experimental.pallas import tpu as pltpu

F32 = jnp.float32
BF16 = jnp.bfloat16

GDN_QK_HEADS = 16
GDN_V_HEADS = 32
GDN_DK = 128
GDN_DV = 128
GDN_CONV = 5
GDN_CHUNK = 64
GDN_Q = GDN_QK_HEADS * GDN_DK
GDN_V = GDN_V_HEADS * GDN_DV
GDN_QKV = 2 * GDN_Q + GDN_V
DIFF_HEADS = 8
DIFF_DH = 128
DIFF_DV = 2 * DIFF_DH
DIFF_Q = DIFF_HEADS * 2 * DIFF_DH
GRID_W = 64
ROPE_BASE = 10000.0
ROPE_AXIS_DIM = DIFF_DH // 2
N_EXPERTS = 8
LN_EPS = 1e-5
RMS_EPS = 1e-6

V7X_VMEM_BYTES = 64 * 1024 * 1024
VMEM_LIMIT = V7X_VMEM_BYTES - 8 * 1024 * 1024
LANES = 128
ROW_TILE = 256
CHUNKS_PER_TILE = ROW_TILE // GDN_CHUNK
MOE_TILE = 512


def _params(*sem):
    return pltpu.CompilerParams(dimension_semantics=sem, vmem_limit_bytes=VMEM_LIMIT)


def _row_tile(m, preferred):
    t = preferred
    while m % t and t > ROW_TILE:
        t //= 2
    assert m % t == 0
    return t


def _dot(a, b):
    return jnp.dot(a, b, preferred_element_type=F32)


def _dot_nt(a, b):
    return lax.dot_general(a, b, (((1,), (1,)), ((), ())), preferred_element_type=F32)


def _dot_tn(a, b):
    return lax.dot_general(a, b, (((0,), (0,)), ((), ())), preferred_element_type=F32)


def _silu(x):
    return x * jax.nn.sigmoid(x)


def _ada_kernel(c_ref, w_ref, b_ref, o_ref):
    cond = _silu(c_ref[...])
    o_ref[0] = jnp.dot(cond, w_ref[0], preferred_element_type=F32,
                       precision=lax.Precision.HIGHEST) + b_ref[0]


def _ada_modulation(cond, ada_w, ada_b):
    depth, d, n = ada_w.shape
    rows = cond.shape[0]
    tn = 1024
    return pl.pallas_call(
        _ada_kernel,
        out_shape=jax.ShapeDtypeStruct((depth, rows, n), F32),
        grid=(depth, n // tn),
        in_specs=[pl.BlockSpec((rows, d), lambda l, j: (0, 0)),
                  pl.BlockSpec((1, d, tn), lambda l, j: (l, 0, j)),
                  pl.BlockSpec((1, 1, tn), lambda l, j: (l, 0, j))],
        out_specs=pl.BlockSpec((1, rows, tn), lambda l, j: (l, 0, j)),
        compiler_params=_params("parallel", "parallel"),
        name="ada_modulation",
    )(cond, ada_w, ada_b.reshape(depth, 1, n))


def _mod_index(r, tiles_per_batch, ctx_tiles, ctx_row):
    return jnp.where(r % tiles_per_batch < ctx_tiles, ctx_row, r // tiles_per_batch)


def _modulate_kernel(x_ref, m_ref, o_ref):
    m = m_ref[0]
    o_ref[...] = (x_ref[...] * (1.0 + m[1:2]) + m[0:1]).astype(o_ref.dtype)


def _modulate(x, mod, tiles_per_batch, ctx_tiles):
    rows, d = x.shape
    ctx_row = mod.shape[0] - 1
    midx = functools.partial(_mod_index, tiles_per_batch=tiles_per_batch,
                             ctx_tiles=ctx_tiles, ctx_row=ctx_row)
    return pl.pallas_call(
        _modulate_kernel,
        out_shape=jax.ShapeDtypeStruct((rows, d), BF16),
        grid=(rows // ROW_TILE,),
        in_specs=[pl.BlockSpec((ROW_TILE, d), lambda r: (r, 0)),
                  pl.BlockSpec((1, 6, d), lambda r: (midx(r), 0, 0))],
        out_specs=pl.BlockSpec((ROW_TILE, d), lambda r: (r, 0)),
        compiler_params=_params("parallel"),
        name="modulate",
    )(x, mod)


def _layer_norm_rows(t, w, b):
    mu = jnp.mean(t, axis=-1, keepdims=True)
    tc = t - mu
    var = jnp.mean(tc * tc, axis=-1, keepdims=True)
    return tc * lax.rsqrt(var + LN_EPS) * w + b


def _res_ln_kernel(x_ref, y_ref, mg_ref, mn_ref, lnw_ref, lnb_ref, xo_ref, ho_ref,
                   *, alpha, gate_idx, shift_idx):
    gate = mg_ref[0][gate_idx:gate_idx + 1]
    t = alpha * x_ref[...] + gate * y_ref[...].astype(F32)
    xn = _layer_norm_rows(t, lnw_ref[...], lnb_ref[...])
    xo_ref[...] = xn
    mn = mn_ref[0]
    h = xn * (1.0 + mn[shift_idx + 1:shift_idx + 2]) + mn[shift_idx:shift_idx + 1]
    ho_ref[...] = h.astype(ho_ref.dtype)


def _res_ln(x, y, mod_gate, gate_idx, mod_next, shift_idx, ln_w, ln_b, alpha,
            tiles_per_batch, ctx_tiles, h_dtype, x_skip_tiles=0):
    rows, d = y.shape
    ctx_row = mod_gate.shape[0] - 1
    x_tiles = tiles_per_batch + x_skip_tiles
    x_spec = pl.BlockSpec(
        (ROW_TILE, d), lambda r: ((r // tiles_per_batch) * x_tiles + x_skip_tiles + r % tiles_per_batch, 0))
    midx = functools.partial(_mod_index, tiles_per_batch=tiles_per_batch,
                             ctx_tiles=ctx_tiles, ctx_row=ctx_row)
    row_spec = pl.BlockSpec((ROW_TILE, d), lambda r: (r, 0))
    mod_spec = pl.BlockSpec((1, 6, d), lambda r: (midx(r), 0, 0))
    vec_spec = pl.BlockSpec((1, d), lambda r: (0, 0))
    return pl.pallas_call(
        functools.partial(_res_ln_kernel, alpha=alpha, gate_idx=gate_idx, shift_idx=shift_idx),
        out_shape=(jax.ShapeDtypeStruct((rows, d), F32), jax.ShapeDtypeStruct((rows, d), h_dtype)),
        grid=(rows // ROW_TILE,),
        in_specs=[x_spec, row_spec, mod_spec, mod_spec, vec_spec, vec_spec],
        out_specs=(row_spec, row_spec),
        compiler_params=_params("parallel"),
        name="residual_layer_norm",
    )(x, y, mod_gate, mod_next, ln_w.reshape(1, d), ln_b.reshape(1, d))


def _mm_kernel(a_ref, w_ref, o_ref, wbf_ref):
    @pl.when(pl.program_id(1) == 0)
    def _():
        wbf_ref[...] = w_ref[...].astype(BF16)

    o_ref[...] = _dot(a_ref[...], wbf_ref[...]).astype(o_ref.dtype)


def _matmul(a, w, *, tm, tn, out_dtype, col_start=0, n_cols=None):
    m, k = a.shape
    tm = _row_tile(m, tm)
    n_cols = w.shape[1] - col_start if n_cols is None else n_cols
    assert m % tm == 0 and n_cols % tn == 0 and col_start % tn == 0
    off = col_start // tn
    return pl.pallas_call(
        _mm_kernel,
        out_shape=jax.ShapeDtypeStruct((m, n_cols), out_dtype),
        grid=(n_cols // tn, m // tm),
        in_specs=[pl.BlockSpec((tm, k), lambda j, i: (i, 0)),
                  pl.BlockSpec((k, tn), lambda j, i: (0, j + off))],
        out_specs=pl.BlockSpec((tm, tn), lambda j, i: (i, j)),
        scratch_shapes=[pltpu.VMEM((k, tn), BF16)],
        compiler_params=_params("parallel", "arbitrary"),
        name="matmul",
    )(a, w)


def _swiglu_up_kernel(a_ref, wg_ref, wu_ref, o_ref, wgb_ref, wub_ref):
    @pl.when(pl.program_id(1) == 0)
    def _():
        wgb_ref[...] = wg_ref[...].astype(BF16)
        wub_ref[...] = wu_ref[...].astype(BF16)

    a = a_ref[...]
    g = _dot(a, wgb_ref[...])
    u = _dot(a, wub_ref[...])
    o_ref[...] = (_silu(g) * u).astype(o_ref.dtype)


def _swiglu_up(a, w_gu, *, tm, tn):
    m, k = a.shape
    tm = _row_tile(m, tm)
    ff = w_gu.shape[1] // 2
    assert m % tm == 0 and ff % tn == 0
    nj = ff // tn
    return pl.pallas_call(
        _swiglu_up_kernel,
        out_shape=jax.ShapeDtypeStruct((m, ff), BF16),
        grid=(nj, m // tm),
        in_specs=[pl.BlockSpec((tm, k), lambda j, i: (i, 0)),
                  pl.BlockSpec((k, tn), lambda j, i: (0, j)),
                  pl.BlockSpec((k, tn), lambda j, i: (0, j + nj))],
        out_specs=pl.BlockSpec((tm, tn), lambda j, i: (i, j)),
        scratch_shapes=[pltpu.VMEM((k, tn), BF16), pltpu.VMEM((k, tn), BF16)],
        compiler_params=_params("parallel", "arbitrary"),
        name="swiglu_up",
    )(a, w_gu, w_gu)


def _gdn_conv_kernel(cur_ref, prev_ref, next_ref, w_ref, o_ref, xe_ref, *, n_qk_blocks, q_scale):
    i = pl.program_id(1)
    j = pl.program_id(2)
    n_tiles = pl.num_programs(1)
    prev_ok = (i >= 2).astype(F32)
    next_ok = jnp.logical_and(i >= 1, i < n_tiles - 1).astype(F32)
    xe_ref[0:8, :] = prev_ref[0].astype(F32)[8:16] * prev_ok
    xe_ref[8:8 + ROW_TILE, :] = cur_ref[0].astype(F32)
    xe_ref[8 + ROW_TILE:16 + ROW_TILE, :] = next_ref[0].astype(F32)[0:8] * next_ok
    pad = (GDN_CONV - 1) // 2
    is_qk = j < n_qk_blocks
    scale = jnp.where(j < n_qk_blocks // 2, q_scale, 1.0)
    for g in range(xe_ref.shape[1] // GDN_DK):
        sl = slice(g * GDN_DK, (g + 1) * GDN_DK)
        xe = xe_ref[:, sl]
        acc = None
        for t in range(GDN_CONV):
            shifted = xe if t == pad else pltpu.roll(xe, (pad - t) % xe.shape[0], axis=0)
            term = shifted[8:8 + ROW_TILE] * w_ref[t:t + 1, sl]
            acc = term if acc is None else acc + term
        y = _silu(acc)
        inv = lax.rsqrt(jnp.sum(y * y, axis=-1, keepdims=True) + 1e-6) * scale
        o_ref[0, :, sl] = (y * jnp.where(is_qk, inv, 1.0)).astype(o_ref.dtype)


def _gdn_conv(pqz, conv_w, batch):
    rows, _ = pqz.shape
    t_all = rows // batch
    x = pqz.reshape(batch, t_all, pqz.shape[1])
    cb = 1024
    n_tiles = t_all // ROW_TILE
    halo = 16
    last_halo = t_all // halo - 1
    per_tile = ROW_TILE // halo
    return pl.pallas_call(
        functools.partial(_gdn_conv_kernel, n_qk_blocks=2 * GDN_Q // cb, q_scale=GDN_DK ** -0.5),
        out_shape=jax.ShapeDtypeStruct((batch, t_all, GDN_QKV), BF16),
        grid=(batch, n_tiles, GDN_QKV // cb),
        in_specs=[pl.BlockSpec((1, ROW_TILE, cb), lambda b, i, j: (b, i, j)),
                  pl.BlockSpec((1, halo, cb), lambda b, i, j: (b, jnp.maximum(i * per_tile - 1, 0), j)),
                  pl.BlockSpec((1, halo, cb),
                               lambda b, i, j: (b, jnp.minimum((i + 1) * per_tile, last_halo), j)),
                  pl.BlockSpec((GDN_CONV, cb), lambda b, i, j: (0, j))],
        out_specs=pl.BlockSpec((1, ROW_TILE, cb), lambda b, i, j: (b, i, j)),
        scratch_shapes=[pltpu.VMEM((ROW_TILE + 16, cb), F32)],
        compiler_params=_params("parallel", "parallel", "parallel"),
        name="gdn_conv",
    )(x, x, x, conv_w)


def _gdn_gate_kernel(ab_ref, alog_ref, dtb_ref, o_ref):
    ab = ab_ref[...]
    half = LANES // 2
    quarter = LANES // 4
    lane = lax.broadcasted_iota(jnp.int32, ab.shape, 1)
    beta = jax.nn.sigmoid(ab)
    x = ab + dtb_ref[...]
    softplus = jnp.maximum(x, 0.0) + jnp.log1p(jnp.exp(-jnp.abs(x)))
    g = jnp.where(lane >= half, -jnp.exp(alog_ref[...]) * softplus, 0.0)
    g = pltpu.roll(g, half, axis=1)
    n = ab.shape[0]
    ri = lax.broadcasted_iota(jnp.int32, (n, n), 0)
    ci = lax.broadcasted_iota(jnp.int32, (n, n), 1)
    same = (ri // GDN_CHUNK) == (ci // GDN_CHUNK)
    hi = lax.Precision.HIGHEST
    m_fwd = jnp.logical_and(same, ci <= ri).astype(F32)
    m_bwd = jnp.logical_and(same, ci >= ri).astype(F32)
    gc_f = jnp.dot(m_fwd, g, preferred_element_type=F32, precision=hi)
    gc_b = jnp.dot(m_bwd, g, preferred_element_type=F32, precision=hi)
    gc = jnp.where(lane < quarter, gc_f, gc_b)
    for d in range(2):
        parts = [pltpu.roll(val, (q * quarter - d * quarter) % LANES, axis=1)
                 for q, val in enumerate((beta, gc))]
        o_ref[d] = jnp.where(lane < quarter, parts[0], jnp.where(lane < 2 * quarter, parts[1], 0.0))


def _gdn_gates(ab, a_log, dt_bias):
    rows = ab.shape[0]
    zeros = jnp.zeros((LANES // 2,), F32)
    alog = jnp.concatenate([zeros, a_log.reshape(-1).astype(F32)]).reshape(1, LANES)
    dtb = jnp.concatenate([zeros, dt_bias.reshape(-1).astype(F32)]).reshape(1, LANES)
    return pl.pallas_call(
        _gdn_gate_kernel,
        out_shape=jax.ShapeDtypeStruct((2, rows, LANES), F32),
        grid=(rows // ROW_TILE,),
        in_specs=[pl.BlockSpec((ROW_TILE, LANES), lambda r: (r, 0)),
                  pl.BlockSpec((1, LANES), lambda r: (0, 0)),
                  pl.BlockSpec((1, LANES), lambda r: (0, 0))],
        out_specs=pl.BlockSpec((2, ROW_TILE, LANES), lambda r: (0, r, 0)),
        compiler_params=_params("parallel"),
        name="gdn_gates",
    )(ab, alog, dtb)


def _block_masks(ri, ci, n):
    masks = [(ri >> 1) == (ci >> 1)]
    for level in range(1, int(math.log2(n))):
        masks.append(jnp.logical_and((ri >> (level + 1)) == (ci >> (level + 1)),
                                     (ri >> level) != (ci >> level)))
    return masks


def _bmm(a, b):
    return jnp.einsum('hij,hjk->hik', a, b, preferred_element_type=F32)


def _bmm_nt(a, b):
    return jnp.einsum('hik,hjk->hij', a, b, preferred_element_type=F32)


def _bmm_tn(a, b):
    return jnp.einsum('hki,hkj->hij', a, b, preferred_element_type=F32)


def _pair_block_diag(x, left):
    return jnp.concatenate([jnp.where(left, x, 0.0), jnp.where(left, 0.0, x)], axis=1).astype(BF16)


def _unit_triangular_inverse(a, eye, masks, left):
    d = eye - jnp.where(masks[0], a, 0.0)
    for mask in masks[1:]:
        ad = _bmm(jnp.where(mask, a, 0.0).astype(BF16), _pair_block_diag(d, left))
        d = d - _bmm(d.astype(BF16), _pair_block_diag(ad, left))
    return d


def _gdn_scan_kernel(q_ref, k_ref, v_ref, g_ref, gr_ref, o_ref, s_ref):
    d = pl.program_id(1)
    s = pl.program_id(2)

    @pl.when(s == 0)
    def _():
        s_ref[...] = jnp.zeros_like(s_ref)

    c = GDN_CHUNK
    n_h = GDN_V_HEADS
    rep = GDN_V_HEADS // GDN_QK_HEADS
    assert rep == 2
    rev = d == 1
    ri = lax.broadcasted_iota(jnp.int32, (c, 2 * c), 0)
    lane = lax.broadcasted_iota(jnp.int32, (c, 2 * c), 1)
    left = lane < c
    ci = lane % c
    later = jnp.where(rev, ci, ri)
    earlier = jnp.where(rev, ri, ci)
    incl = later >= earlier
    strict = later > earlier
    eye = (ri == ci).astype(F32)
    masks = _block_masks(ri, ci, c)

    def split_pairs(x):
        return jnp.stack([x[j, i * c:(i + 1) * c] for j in range(GDN_QK_HEADS) for i in range(rep)])

    def prepare(cc):
        off = pl.multiple_of(cc * c, c)
        q_all = q_ref[0, pl.ds(off, c), :]
        k_all = k_ref[0, pl.ds(off, c), :]
        v_all = v_ref[0, pl.ds(off, c), :]
        gates = g_ref[0, pl.ds(off, c), :]
        gc_rows = gr_ref[0, 0, cc]
        qs = [q_all[:, i * GDN_DK:(i + 1) * GDN_DK] for i in range(GDN_QK_HEADS)]
        ks = [k_all[:, i * GDN_DK:(i + 1) * GDN_DK] for i in range(GDN_QK_HEADS)]
        q_over_k = jnp.stack([jnp.concatenate([qs[i], ks[i]], axis=0) for i in range(GDN_QK_HEADS)])
        k_twice = jnp.stack([jnp.concatenate([ks[i], ks[i]], axis=0) for i in range(GDN_QK_HEADS)])
        qk_kk = _bmm_nt(q_over_k, k_twice)

        def col(quantity, h):
            idx = quantity * n_h + h
            return gates[:, idx:idx + 1]

        a_l, qk_l, rhs_l, qg_l, kt_l, egl_l = [], [], [], [], [], []
        for j in range(GDN_QK_HEADS):
            kf = ks[j].astype(F32)
            qf = qs[j].astype(F32)
            rhs_pair, beta_pair, gc_pair = [], [], []
            for h in (rep * j, rep * j + 1):
                beta = jnp.broadcast_to(col(0, h), (c, 2 * c))
                gc = jnp.broadcast_to(col(1, h), (c, 2 * c))
                g_total = jnp.where(rev, gc[0:1, :], gc[c - 1:c, :])
                eg = jnp.exp(gc)
                ek = jnp.exp(g_total - gc)
                vf = v_all[:, h * GDN_DV:(h + 1) * GDN_DV].astype(F32)
                rhs_pair.append(jnp.concatenate([vf * beta, kf * (beta * eg)], axis=1).astype(BF16))
                qg_l.append((qf * eg).astype(BF16))
                kt_l.append((kf * ek).T.astype(BF16))
                egl_l.append(jnp.exp(g_total))
                beta_pair.append(beta)
                gc_pair.append(gc)
            rhs_l.append(jnp.concatenate(rhs_pair, axis=0))
            beta_p = jnp.where(left, beta_pair[0], beta_pair[1])
            gc_p = jnp.where(left, gc_pair[0], gc_pair[1])
            gdiff = gc_p - gc_rows[j:j + 1, :]
            decay = jnp.where(incl, jnp.exp(jnp.where(incl, gdiff, 0.0)), 0.0)
            a_l.append(jnp.where(strict, qk_kk[j, c:] * (beta_p * decay), 0.0))
            qk_l.append(qk_kk[j, 0:c] * decay)
        return a_l, qk_l, rhs_l, qg_l, kt_l, egl_l

    def chunk_step(position, carry):
        cc = jnp.where(rev, CHUNKS_PER_TILE - 1 - position, position)
        a_v, qk_v, rhs_v, qg_v, kt_v, egl_v = (jnp.stack(parts) for parts in prepare(cc))
        t_inv = _unit_triangular_inverse(a_v, eye, masks, left)
        uw = split_pairs(_bmm(_pair_block_diag(t_inv, left), rhs_v))
        u = uw[:, :, 0:GDN_DV]
        w = uw[:, :, GDN_DV:]
        state = s_ref[...]
        state_b = state.astype(BF16)
        w_qg = _bmm(jnp.concatenate([w.astype(BF16), qg_v], axis=1), state_b)
        v_new = u - w_qg[:, 0:c]
        v_new_b = v_new.astype(BF16)
        v_pairs = jnp.stack([jnp.concatenate([v_new_b[rep * j], v_new_b[rep * j + 1]], axis=0)
                             for j in range(GDN_QK_HEADS)])
        o = w_qg[:, c:] + split_pairs(_bmm(_pair_block_diag(qk_v, left), v_pairs))
        s_ref[...] = state * egl_v + _bmm(kt_v, v_new_b)
        off = pl.multiple_of(cc * c, c)
        for h in range(n_h):
            o_ref[0, 0, pl.ds(off, c), h * GDN_DV:(h + 1) * GDN_DV] = o[h].astype(o_ref.dtype)
        return carry

    lax.fori_loop(0, CHUNKS_PER_TILE, chunk_step, 0, unroll=2)


def _gdn_scan(qkv, gates, gc_rows, batch):
    _, t_all, _ = qkv.shape
    n_tiles = t_all // ROW_TILE

    def tile(d, s):
        return jnp.where(d == 0, s, jnp.where(s == 0, 0, n_tiles - s))

    return pl.pallas_call(
        _gdn_scan_kernel,
        out_shape=jax.ShapeDtypeStruct((2, batch, t_all, GDN_V), BF16),
        grid=(batch, 2, n_tiles),
        in_specs=[
            pl.BlockSpec((1, ROW_TILE, GDN_Q), lambda b, d, s: (b, tile(d, s), 0)),
            pl.BlockSpec((1, ROW_TILE, GDN_Q), lambda b, d, s: (b, tile(d, s), 1)),
            pl.BlockSpec((1, ROW_TILE, GDN_V), lambda b, d, s: (b, tile(d, s), 1)),
            pl.BlockSpec((1, ROW_TILE, LANES), lambda b, d, s: (d, b * n_tiles + tile(d, s), 0)),
            pl.BlockSpec((1, 1, CHUNKS_PER_TILE, GDN_QK_HEADS, 2 * GDN_CHUNK),
                         lambda b, d, s: (b, d, tile(d, s), 0, 0)),
        ],
        out_specs=pl.BlockSpec((1, 1, ROW_TILE, GDN_V), lambda b, d, s: (d, b, tile(d, s), 0)),
        scratch_shapes=[pltpu.VMEM((GDN_V_HEADS, GDN_DK, GDN_DV), F32)],
        compiler_params=_params("parallel", "arbitrary", "arbitrary"),
        name="gdn_scan",
    )(qkv, qkv, qkv, gates, gc_rows)


def _gdn_norm_gate_kernel(of_ref, ob_ref, z_ref, w_ref, o_ref):
    w = w_ref[...]
    for g in range(o_ref.shape[1] // GDN_DV):
        sl = slice(g * GDN_DV, (g + 1) * GDN_DV)
        og = of_ref[0, :, sl].astype(F32) + ob_ref[0, :, sl].astype(F32)
        inv = lax.rsqrt(jnp.mean(og * og, axis=-1, keepdims=True) + RMS_EPS)
        o_ref[:, sl] = (og * inv * w * _silu(z_ref[:, sl].astype(F32))).astype(o_ref.dtype)


def _gdn_norm_gate(o2, pqz, norm_w):
    _, rows, width = o2.shape
    cb = width
    z_off = GDN_QKV // cb
    return pl.pallas_call(
        _gdn_norm_gate_kernel,
        out_shape=jax.ShapeDtypeStruct((rows, width), BF16),
        grid=(rows // ROW_TILE, width // cb),
        in_specs=[pl.BlockSpec((1, ROW_TILE, cb), lambda r, j: (0, r, j)),
                  pl.BlockSpec((1, ROW_TILE, cb), lambda r, j: (1, r, j)),
                  pl.BlockSpec((ROW_TILE, cb), lambda r, j: (r, z_off + j)),
                  pl.BlockSpec((1, GDN_DV), lambda r, j: (0, 0))],
        out_specs=pl.BlockSpec((ROW_TILE, cb), lambda r, j: (r, j)),
        compiler_params=_params("parallel", "parallel"),
        name="gdn_norm_gate",
    )(o2, o2, pqz, norm_w.reshape(1, GDN_DV))


def _gdn_mixer(h, batch, w_in, conv_w, a_log, dt_bias, norm_w, w_out):
    rows = h.shape[0]
    t_all = rows // batch
    pqz = _matmul(h, w_in, tm=1024, tn=1024, out_dtype=BF16, col_start=0, n_cols=GDN_QKV + GDN_V)
    ab = _matmul(h, w_in, tm=1024, tn=LANES, out_dtype=F32, col_start=GDN_QKV + GDN_V, n_cols=LANES)
    gates = _gdn_gates(ab, a_log, dt_bias)
    n_chunks = t_all // GDN_CHUNK
    gc_rows = gates[:, :, GDN_V_HEADS:2 * GDN_V_HEADS]
    gc_rows = gc_rows.reshape(2, batch, n_chunks, GDN_CHUNK, GDN_V_HEADS).transpose(1, 0, 2, 4, 3)
    gc_rows = gc_rows.reshape(batch, 2, n_chunks, GDN_QK_HEADS, 2 * GDN_CHUNK)
    qkv = _gdn_conv(pqz, conv_w, batch)
    o2 = _gdn_scan(qkv, gates, gc_rows, batch)
    og = _gdn_norm_gate(o2.reshape(2, rows, GDN_V), pqz, norm_w)
    return _matmul(og, w_out, tm=1024, tn=512, out_dtype=F32)


def _rope_tables(n_ctx, n_lat):
    rows = n_lat // GRID_W
    row = jnp.repeat(jnp.arange(rows, dtype=F32), GRID_W)
    col = jnp.tile(jnp.arange(GRID_W, dtype=F32), rows)
    inv_freq = 1.0 / (ROPE_BASE ** (jnp.arange(0, ROPE_AXIS_DIM, 2, dtype=F32) / ROPE_AXIS_DIM))
    ang_r = row[:, None] * inv_freq
    ang_c = col[:, None] * inv_freq
    cos = jnp.concatenate([jnp.cos(ang_r), jnp.cos(ang_r), jnp.cos(ang_c), jnp.cos(ang_c)], axis=1)
    sin = jnp.concatenate([-jnp.sin(ang_r), jnp.sin(ang_r), -jnp.sin(ang_c), jnp.sin(ang_c)], axis=1)
    cos = jnp.concatenate([jnp.ones((n_ctx, DIFF_DH), F32), cos], axis=0)
    sin = jnp.concatenate([jnp.zeros((n_ctx, DIFF_DH), F32), sin], axis=0)
    return cos, sin


def _qkv_rope_kernel(a_ref, w_ref, cos_ref, sin_ref, o_ref, wbf_ref, *, n_rope_blocks, n_q_blocks,
                     q_scale):
    j = pl.program_id(0)

    @pl.when(pl.program_id(1) == 0)
    def _():
        wbf_ref[...] = w_ref[...].astype(BF16)

    acc = _dot(a_ref[...], wbf_ref[...])

    @pl.when(j >= n_rope_blocks)
    def _():
        o_ref[...] = acc.astype(o_ref.dtype)

    @pl.when(j < n_rope_blocks)
    def _():
        scale = jnp.where(j < n_q_blocks, q_scale, 1.0)
        cos = cos_ref[...] * scale
        sin = sin_ref[...] * scale
        quarter = ROPE_AXIS_DIM // 2
        lane = lax.broadcasted_iota(jnp.int32, cos.shape, 1)
        first = (lane % ROPE_AXIS_DIM) < quarter
        for g in range(acc.shape[1] // DIFF_DH):
            sl = slice(g * DIFF_DH, (g + 1) * DIFF_DH)
            xg = acc[:, sl]
            partner = jnp.where(first, pltpu.roll(xg, DIFF_DH - quarter, axis=1),
                                pltpu.roll(xg, quarter, axis=1))
            o_ref[:, sl] = (xg * cos + partner * sin).astype(o_ref.dtype)


def _qkv_rope(a, w, cos, sin, *, tm, tn, q_scale):
    m, k = a.shape
    tm = _row_tile(m, tm)
    n = w.shape[1]
    assert n % tn == 0 and DIFF_Q % tn == 0
    return pl.pallas_call(
        functools.partial(_qkv_rope_kernel, n_rope_blocks=2 * DIFF_Q // tn, n_q_blocks=DIFF_Q // tn,
                          q_scale=q_scale),
        out_shape=jax.ShapeDtypeStruct((m, n), BF16),
        grid=(n // tn, m // tm),
        in_specs=[pl.BlockSpec((tm, k), lambda j, i: (i, 0)),
                  pl.BlockSpec((k, tn), lambda j, i: (0, j)),
                  pl.BlockSpec((tm, DIFF_DH), lambda j, i: (i, 0)),
                  pl.BlockSpec((tm, DIFF_DH), lambda j, i: (i, 0))],
        out_specs=pl.BlockSpec((tm, tn), lambda j, i: (i, j)),
        scratch_shapes=[pltpu.VMEM((k, tn), BF16)],
        compiler_params=_params("parallel", "arbitrary"),
        name="qkv_rope",
    )(a, w, cos, sin)


def _diff_attn_kernel(lam_ref, nw_ref, q_ref, k_ref, v_ref, o_ref, *, lam_init, key_chunk):
    lp = lam_ref[...]
    lam = (jnp.exp(jnp.sum(lp[0:1] * lp[1:2], axis=-1, keepdims=True))
           - jnp.exp(jnp.sum(lp[2:3] * lp[3:4], axis=-1, keepdims=True)) + lam_init)
    q = q_ref[0]
    n_keys = k_ref.shape[1]

    def attend(m):
        sl = slice(m * DIFF_DH, (m + 1) * DIFF_DH)
        qm = q[:, sl]
        m_run = l_run = acc = None
        for c in range(pl.cdiv(n_keys, key_chunk)):
            rows = slice(c * key_chunk, min((c + 1) * key_chunk, n_keys))
            s2 = _dot_nt(qm, k_ref[0, rows, sl])
            m_c = jnp.max(s2, axis=-1, keepdims=True)
            if c == 0:
                m_new = m_c
                p = jnp.exp2(s2 - m_new)
                l_run = jnp.sum(p, axis=-1, keepdims=True)
                acc = _dot(p.astype(BF16), v_ref[0, rows, :])
            else:
                m_new = jnp.maximum(m_run, m_c)
                rescale = jnp.exp2(m_run - m_new)
                p = jnp.exp2(s2 - m_new)
                l_run = l_run * rescale + jnp.sum(p, axis=-1, keepdims=True)
                acc = acc * rescale + _dot(p.astype(BF16), v_ref[0, rows, :])
            m_run = m_new
        return acc, l_run

    o1, l1 = attend(0)
    o2, l2 = attend(1)
    o = o1 * (1.0 / l1) - o2 * (lam / l2)
    inv = lax.rsqrt(jnp.mean(o * o, axis=-1, keepdims=True) + RMS_EPS)
    o_ref[0] = (o * inv * nw_ref[...] * (1.0 - lam_init)).astype(o_ref.dtype)


def _diff_attention(qkv, batch, n_ctx, lam_p, norm_w, lam_init):
    t_all = qkv.shape[0] // batch
    n_lat = t_all - n_ctx
    tq = ROW_TILE
    assert n_ctx % tq == 0
    ctx_tiles = n_ctx // tq
    v3 = qkv.reshape(batch, t_all, qkv.shape[1])
    k_off = DIFF_Q // DIFF_DV
    v_off = 2 * DIFF_Q // DIFF_DV
    return pl.pallas_call(
        functools.partial(_diff_attn_kernel, lam_init=lam_init, key_chunk=2 * ROW_TILE),
        out_shape=jax.ShapeDtypeStruct((batch, n_lat, DIFF_HEADS * DIFF_DV), BF16),
        grid=(batch, DIFF_HEADS, n_lat // tq),
        in_specs=[pl.BlockSpec((4, DIFF_DH), lambda b, h, i: (0, 0)),
                  pl.BlockSpec((1, DIFF_DV), lambda b, h, i: (0, 0)),
                  pl.BlockSpec((1, tq, DIFF_DV), lambda b, h, i: (b, i + ctx_tiles, h)),
                  pl.BlockSpec((1, t_all, DIFF_DV), lambda b, h, i: (b, 0, k_off + h)),
                  pl.BlockSpec((1, t_all, DIFF_DV), lambda b, h, i: (b, 0, v_off + h))],
        out_specs=pl.BlockSpec((1, tq, DIFF_DV), lambda b, h, i: (b, i, h)),
        compiler_params=_params("parallel", "parallel", "arbitrary"),
        name="diff_attention",
    )(lam_p.astype(F32), norm_w.reshape(1, DIFF_DV), v3, v3, v3)


def _router_kernel(h_ref, r_ref, idx_ref, w_ref):
    logits = _dot_nt(r_ref[...].astype(BF16), h_ref[...].astype(BF16))
    e = lax.broadcasted_iota(jnp.int32, logits.shape, 0)
    n_e = logits.shape[0]
    m1 = jnp.max(logits, axis=0, keepdims=True)
    i1 = jnp.min(jnp.where(logits == m1, e, n_e), axis=0, keepdims=True)
    rest = jnp.where(e == i1, -jnp.inf, logits)
    m2 = jnp.max(rest, axis=0, keepdims=True)
    i2 = jnp.min(jnp.where(rest == m2, e, n_e), axis=0, keepdims=True)
    t = jnp.exp(m2 - m1)
    idx_ref[0:1, :] = i1
    idx_ref[1:2, :] = i2
    w_ref[0:1, :] = 1.0 / (1.0 + t)
    w_ref[1:2, :] = t / (1.0 + t)


def _router(h, router):
    rows, d = h.shape
    tm = _row_tile(rows, 1024)
    return pl.pallas_call(
        _router_kernel,
        out_shape=(jax.ShapeDtypeStruct((2, rows), jnp.int32), jax.ShapeDtypeStruct((2, rows), F32)),
        grid=(rows // tm,),
        in_specs=[pl.BlockSpec((tm, d), lambda i: (i, 0)),
                  pl.BlockSpec((N_EXPERTS, d), lambda i: (0, 0))],
        out_specs=(pl.BlockSpec((2, tm), lambda i: (0, i)), pl.BlockSpec((2, tm), lambda i: (0, i))),
        compiler_params=_params("parallel"),
        name="moe_router",
    )(h, router.T)


def _row_copy(src_hbm, dst_vmem, sem, src_row, dst_row):
    return pltpu.make_async_copy(src_hbm.at[pl.ds(src_row, 1)], dst_vmem.at[pl.ds(dst_row, 1)], sem)


def _gather_rows_kernel(idx_ref, src_ref, o_ref, buf_ref, sem):
    n = o_ref.shape[0]
    i = pl.program_id(0)
    slot = i % 2

    def issue_tile(tile, dst_slot):
        def body(r2, carry):
            for p in range(2):
                r = 2 * r2 + p
                _row_copy(src_ref, buf_ref.at[dst_slot], sem.at[dst_slot],
                          idx_ref[tile * n + r], r).start(priority=p)
            return carry
        lax.fori_loop(0, n // 2, body, 0, unroll=4)

    @pl.when(i == 0)
    def _():
        issue_tile(0, 0)

    @pl.when(i + 1 < pl.num_programs(0))
    def _():
        issue_tile(i + 1, 1 - slot)

    def wait_row(r, carry):
        _row_copy(src_ref, buf_ref.at[slot], sem.at[slot], 0, r).wait()
        return carry

    lax.fori_loop(0, n, wait_row, 0, unroll=8)
    o_ref[...] = buf_ref[slot].astype(o_ref.dtype)


def _gather_rows(src, idx, out_dtype):
    n = idx.shape[0]
    d = src.shape[1]
    tg = ROW_TILE
    return pl.pallas_call(
        _gather_rows_kernel,
        out_shape=jax.ShapeDtypeStruct((n, d), out_dtype),
        grid_spec=pltpu.PrefetchScalarGridSpec(
            num_scalar_prefetch=1,
            grid=(n // tg,),
            in_specs=[pl.BlockSpec(memory_space=pl.ANY)],
            out_specs=pl.BlockSpec((tg, d), lambda i, idx: (i, 0)),
            scratch_shapes=[pltpu.VMEM((2, tg, d), src.dtype), pltpu.SemaphoreType.DMA((2,))]),
        compiler_params=_params("arbitrary"),
        name="moe_gather",
    )(idx, src)


def _expert_up_kernel(te_ref, nt_ref, a_ref, wg_ref, wu_ref, o_ref, wgb_ref, wub_ref):
    i = pl.program_id(1)
    new_expert = jnp.logical_or(i == 0, te_ref[i] != te_ref[jnp.maximum(i - 1, 0)])

    @pl.when(new_expert)
    def _():
        wgb_ref[...] = wg_ref[0].astype(BF16)
        wub_ref[...] = wu_ref[0].astype(BF16)

    @pl.when(i < nt_ref[0])
    def _():
        a = a_ref[...]
        g = _dot(a, wgb_ref[...])
        u = _dot(a, wub_ref[...])
        o_ref[...] = (_silu(g) * u).astype(o_ref.dtype)

    @pl.when(i >= nt_ref[0])
    def _():
        o_ref[...] = jnp.zeros_like(o_ref)


def _expert_up(xs, tile_expert, n_tiles_used, w_gu, *, tn):
    p, k = xs.shape
    ff = w_gu.shape[2] // 2
    nj = ff // tn
    tm = MOE_TILE
    return pl.pallas_call(
        _expert_up_kernel,
        out_shape=jax.ShapeDtypeStruct((p, ff), BF16),
        grid_spec=pltpu.PrefetchScalarGridSpec(
            num_scalar_prefetch=2,
            grid=(nj, p // tm),
            in_specs=[pl.BlockSpec((tm, k), lambda j, i, te, nt: (i, 0)),
                      pl.BlockSpec((1, k, tn), lambda j, i, te, nt: (te[i], 0, j)),
                      pl.BlockSpec((1, k, tn), lambda j, i, te, nt: (te[i], 0, j + nj))],
            out_specs=pl.BlockSpec((tm, tn), lambda j, i, te, nt: (i, j)),
            scratch_shapes=[pltpu.VMEM((k, tn), BF16), pltpu.VMEM((k, tn), BF16)]),
        compiler_params=_params("parallel", "arbitrary"),
        name="moe_expert_up",
    )(tile_expert, n_tiles_used, xs, w_gu, w_gu)


def _expert_down_kernel(te_ref, nt_ref, a_ref, w_ref, o_ref, wb_ref):
    i = pl.program_id(1)
    new_expert = jnp.logical_or(i == 0, te_ref[i] != te_ref[jnp.maximum(i - 1, 0)])

    @pl.when(new_expert)
    def _():
        wb_ref[...] = w_ref[0].astype(BF16)

    @pl.when(i < nt_ref[0])
    def _():
        o_ref[...] = _dot(a_ref[...], wb_ref[...]).astype(o_ref.dtype)

    @pl.when(i >= nt_ref[0])
    def _():
        o_ref[...] = jnp.zeros_like(o_ref)


def _expert_down(act, tile_expert, n_tiles_used, w_down, *, tn):
    p, k = act.shape
    n = w_down.shape[2]
    tm = MOE_TILE
    return pl.pallas_call(
        _expert_down_kernel,
        out_shape=jax.ShapeDtypeStruct((p, n), F32),
        grid_spec=pltpu.PrefetchScalarGridSpec(
            num_scalar_prefetch=2,
            grid=(n // tn, p // tm),
            in_specs=[pl.BlockSpec((tm, k), lambda j, i, te, nt: (i, 0)),
                      pl.BlockSpec((1, k, tn), lambda j, i, te, nt: (te[i], 0, j))],
            out_specs=pl.BlockSpec((tm, tn), lambda j, i, te, nt: (i, j)),
            scratch_shapes=[pltpu.VMEM((k, tn), BF16)]),
        compiler_params=_params("parallel", "arbitrary"),
        name="moe_expert_down",
    )(tile_expert, n_tiles_used, act, w_down)


def _combine_ln_kernel(slot_ref, ys_ref, w_ref, x_ref, mg_ref, lnw_ref, lnb_ref, o_ref,
                       buf_ref, sem, *, alpha, gate_idx):
    n = x_ref.shape[0]
    n_tok = n * pl.num_programs(0)
    i = pl.program_id(0)
    slot = i % 2

    def issue_tile(tile, dst_slot):
        def body(r, carry):
            for choice in range(2):
                _row_copy(ys_ref, buf_ref.at[dst_slot, choice], sem.at[dst_slot],
                          slot_ref[choice * n_tok + tile * n + r], r).start(priority=choice)
            return carry
        lax.fori_loop(0, n, body, 0, unroll=4)

    @pl.when(i == 0)
    def _():
        issue_tile(0, 0)

    @pl.when(i + 1 < pl.num_programs(0))
    def _():
        issue_tile(i + 1, 1 - slot)

    def wait_row(r, carry):
        for choice in range(2):
            _row_copy(ys_ref, buf_ref.at[slot, choice], sem.at[slot], 0, r).wait()
        return carry

    lax.fori_loop(0, n, wait_row, 0, unroll=4)
    w = w_ref[...]
    f = w[:, 0:1] * buf_ref[slot, 0] + w[:, 1:2] * buf_ref[slot, 1]
    gate = mg_ref[0][gate_idx:gate_idx + 1]
    t = alpha * x_ref[...] + gate * f
    o_ref[...] = _layer_norm_rows(t, lnw_ref[...], lnb_ref[...])


def _combine_ln(ys, slots, weights, x, mod_gate, gate_idx, ln_w, ln_b, alpha, tiles_per_batch):
    rows, d = x.shape
    tc = ROW_TILE
    row_spec = pl.BlockSpec((tc, d), lambda r, s: (r, 0))
    vec_spec = pl.BlockSpec((1, d), lambda r, s: (0, 0))
    return pl.pallas_call(
        functools.partial(_combine_ln_kernel, alpha=alpha, gate_idx=gate_idx),
        out_shape=jax.ShapeDtypeStruct((rows, d), F32),
        grid_spec=pltpu.PrefetchScalarGridSpec(
            num_scalar_prefetch=1,
            grid=(rows // tc,),
            in_specs=[pl.BlockSpec(memory_space=pl.ANY),
                      pl.BlockSpec((tc, 2), lambda r, s: (r, 0)),
                      row_spec,
                      pl.BlockSpec((1, 6, d), lambda r, s: (r // tiles_per_batch, 0, 0)),
                      vec_spec, vec_spec],
            out_specs=row_spec,
            scratch_shapes=[pltpu.VMEM((2, 2, tc, d), F32), pltpu.SemaphoreType.DMA((2,))]),
        compiler_params=_params("arbitrary"),
        name="moe_combine_layer_norm",
    )(slots, ys, weights, x, mod_gate, ln_w.reshape(1, d), ln_b.reshape(1, d))


def _routing_tables(top_i, n_slots):
    n_tok = top_i.shape[1]
    tm = MOE_TILE
    pair_e = top_i.T.reshape(-1)
    onehot = (pair_e[:, None] == jnp.arange(N_EXPERTS, dtype=jnp.int32)[None, :]).astype(jnp.int32)
    rank = jnp.cumsum(onehot, axis=0) - onehot
    counts = jnp.sum(onehot, axis=0)
    padded = ((counts + tm - 1) // tm) * tm
    ends = jnp.cumsum(padded)
    starts = ends - padded
    slot = jnp.sum(onehot * (starts[None, :] + rank), axis=1)
    token_of_slot = jnp.zeros((n_slots,), jnp.int32).at[slot].set(
        jnp.arange(2 * n_tok, dtype=jnp.int32) // 2)
    tile_start = jnp.arange(n_slots // tm, dtype=jnp.int32) * tm
    tile_expert = jnp.minimum(
        jnp.sum((tile_start[:, None] >= ends[None, :]).astype(jnp.int32), axis=1), N_EXPERTS - 1)
    n_tiles_used = (ends[-1:] // tm).astype(jnp.int32)
    slots = slot.reshape(n_tok, 2).T.reshape(-1)
    return slots, token_of_slot, tile_expert.astype(jnp.int32), n_tiles_used


def _moe(h_f32, router, w_gu, w_down):
    n_tok = h_f32.shape[0]
    top_i, top_w = _router(h_f32, router)
    n_slots = 2 * n_tok + N_EXPERTS * MOE_TILE
    slots, token_of_slot, tile_expert, n_tiles_used = _routing_tables(top_i, n_slots)
    xs = _gather_rows(h_f32, token_of_slot, BF16)
    act = _expert_up(xs, tile_expert, n_tiles_used, w_gu, tn=1024)
    ys = _expert_down(act, tile_expert, n_tiles_used, w_down, tn=512)
    return ys, slots, top_w.T


def kernel(x, c, ctx, c_ctx, ada_w, ada_b, ln_w, ln_b, gdn_w_in, gdn_conv_w, gdn_a_log, gdn_dt_bias,
           gdn_norm_w, gdn_w_out, ffn_w_gu, ffn_w_down, diff_w_in, diff_lambda, diff_norm_w,
           diff_w_out, moe_router, moe_w_gu, moe_w_down):
    batch, n_lat, d = x.shape
    n_ctx = ctx.shape[1]
    depth = ada_w.shape[0]
    assert depth == 2 and n_ctx == ROW_TILE and n_lat % ROW_TILE == 0
    t_all = n_ctx + n_lat
    alpha = (2.0 * depth) ** 0.25
    tiles_all = t_all // ROW_TILE
    tiles_lat = n_lat // ROW_TILE

    n_cond = batch + 1
    cond = jnp.concatenate([c, c_ctx[None, :], jnp.zeros((-n_cond % 8, d), F32)], axis=0)
    mod = _ada_modulation(cond, ada_w, ada_b)[:, :n_cond].reshape(depth, n_cond, 6, d)

    xa = jnp.concatenate([ctx, x], axis=1).reshape(batch * t_all, d)

    h = _modulate(xa, mod[0], tiles_all, 1)
    y = _gdn_mixer(h, batch, gdn_w_in[0], gdn_conv_w[0], gdn_a_log[0], gdn_dt_bias[0],
                   gdn_norm_w[0], gdn_w_out[0])
    xa, h = _res_ln(xa, y, mod[0], 2, mod[0], 3, ln_w[0, 0], ln_b[0, 0], alpha, tiles_all, 1, BF16)
    act = _swiglu_up(h, ffn_w_gu[0], tm=1024, tn=512)
    f = _matmul(act, ffn_w_down[0], tm=512, tn=512, out_dtype=F32)
    xa, h = _res_ln(xa, f, mod[0], 5, mod[1], 0, ln_w[0, 1], ln_b[0, 1], alpha, tiles_all, 1, BF16)

    lam_init = 0.8 - 0.6 * math.exp(-0.3 * 1)
    cos, sin = _rope_tables(n_ctx, n_lat)
    qkv = _qkv_rope(h, diff_w_in[0], jnp.tile(cos, (batch, 1)), jnp.tile(sin, (batch, 1)),
                    tm=1024, tn=1024, q_scale=DIFF_DH ** -0.5 * math.log2(math.e))
    o = _diff_attention(qkv, batch, n_ctx, diff_lambda[0], diff_norm_w[0], lam_init)
    y = _matmul(o.reshape(batch * n_lat, DIFF_HEADS * DIFF_DV), diff_w_out[0],
                tm=1024, tn=1024, out_dtype=F32)
    xl, hf = _res_ln(xa, y, mod[1], 2, mod[1], 3, ln_w[1, 0], ln_b[1, 0], alpha, tiles_lat, 0, F32,
                     x_skip_tiles=tiles_all - tiles_lat)
    ys, slots, top_w = _moe(hf, moe_router[0], moe_w_gu[0], moe_w_down[0])
    out = _combine_ln(ys, slots, top_w, xl, mod[1], 5, ln_w[1, 1], ln_b[1, 1], alpha, tiles_lat)
    return out.reshape(batch, n_lat, d)
```

```python
import functools
import math

import jax
import jax.numpy as jnp
from jax import lax
from jax.experimental import pallas as pl
from jax.experimental.pallas import tpu as pltpu

F32 = jnp.float32
BF16 = jnp.bfloat16

GDN_QK_HEADS = 16
GDN_V_HEADS = 32
GDN_DK = 128
GDN_DV = 128
GDN_CONV = 5
GDN_CHUNK = 64
GDN_Q = GDN_QK_HEADS * GDN_DK
GDN_V = GDN_V_HEADS * GDN_DV
GDN_QKV = 2 * GDN_Q + GDN_V
DIFF_HEADS = 8
DIFF_DH = 128
DIFF_DV = 2 * DIFF_DH
DIFF_Q = DIFF_HEADS * 2 * DIFF_DH
GRID_W = 64
ROPE_BASE = 10000.0
ROPE_AXIS_DIM = DIFF_DH // 2
N_EXPERTS = 8
LN_EPS = 1e-5
RMS_EPS = 1e-6

V7X_VMEM_BYTES = 64 * 1024 * 1024
VMEM_LIMIT = V7X_VMEM_BYTES - 8 * 1024 * 1024
LANES = 128
ROW_TILE = 256
CHUNKS_PER_TILE = ROW_TILE // GDN_CHUNK
MOE_TILE = 512


def _params(*sem):
    return pltpu.CompilerParams(dimension_semantics=sem, vmem_limit_bytes=VMEM_LIMIT)


def _row_tile(m, preferred):
    t = preferred
    while m % t and t > ROW_TILE:
        t //= 2
    assert m % t == 0
    return t


def _dot(a, b):
    return jnp.dot(a, b, preferred_element_type=F32)


def _dot_nt(a, b):
    return lax.dot_general(a, b, (((1,), (1,)), ((), ())), preferred_element_type=F32)


def _dot_tn(a, b):
    return lax.dot_general(a, b, (((0,), (0,)), ((), ())), preferred_element_type=F32)


def _silu(x):
    return x * jax.nn.sigmoid(x)


def _ada_kernel(c_ref, w_ref, b_ref, o_ref):
    cond = _silu(c_ref[...])
    o_ref[0] = jnp.dot(cond, w_ref[0], preferred_element_type=F32,
                       precision=lax.Precision.HIGHEST) + b_ref[0]


def _ada_modulation(cond, ada_w, ada_b):
    depth, d, n = ada_w.shape
    rows = cond.shape[0]
    tn = 1024
    return pl.pallas_call(
        _ada_kernel,
        out_shape=jax.ShapeDtypeStruct((depth, rows, n), F32),
        grid=(depth, n // tn),
        in_specs=[pl.BlockSpec((rows, d), lambda l, j: (0, 0)),
                  pl.BlockSpec((1, d, tn), lambda l, j: (l, 0, j)),
                  pl.BlockSpec((1, 1, tn), lambda l, j: (l, 0, j))],
        out_specs=pl.BlockSpec((1, rows, tn), lambda l, j: (l, 0, j)),
        compiler_params=_params("parallel", "parallel"),
        name="ada_modulation",
    )(cond, ada_w, ada_b.reshape(depth, 1, n))


def _mod_index(r, tiles_per_batch, ctx_tiles, ctx_row):
    return jnp.where(r % tiles_per_batch < ctx_tiles, ctx_row, r // tiles_per_batch)


def _modulate_kernel(x_ref, m_ref, o_ref):
    m = m_ref[0]
    o_ref[...] = (x_ref[...] * (1.0 + m[1:2]) + m[0:1]).astype(o_ref.dtype)


def _modulate(x, mod, tiles_per_batch, ctx_tiles):
    rows, d = x.shape
    ctx_row = mod.shape[0] - 1
    midx = functools.partial(_mod_index, tiles_per_batch=tiles_per_batch,
                             ctx_tiles=ctx_tiles, ctx_row=ctx_row)
    return pl.pallas_call(
        _modulate_kernel,
        out_shape=jax.ShapeDtypeStruct((rows, d), BF16),
        grid=(rows // ROW_TILE,),
        in_specs=[pl.BlockSpec((ROW_TILE, d), lambda r: (r, 0)),
                  pl.BlockSpec((1, 6, d), lambda r: (midx(r), 0, 0))],
        out_specs=pl.BlockSpec((ROW_TILE, d), lambda r: (r, 0)),
        compiler_params=_params("parallel"),
        name="modulate",
    )(x, mod)


def _layer_norm_rows(t, w, b):
    mu = jnp.mean(t, axis=-1, keepdims=True)
    tc = t - mu
    var = jnp.mean(tc * tc, axis=-1, keepdims=True)
    return tc * lax.rsqrt(var + LN_EPS) * w + b


def _res_ln_kernel(x_ref, y_ref, mg_ref, mn_ref, lnw_ref, lnb_ref, xo_ref, ho_ref,
                   *, alpha, gate_idx, shift_idx):
    gate = mg_ref[0][gate_idx:gate_idx + 1]
    t = alpha * x_ref[...] + gate * y_ref[...].astype(F32)
    xn = _layer_norm_rows(t, lnw_ref[...], lnb_ref[...])
    xo_ref[...] = xn
    mn = mn_ref[0]
    h = xn * (1.0 + mn[shift_idx + 1:shift_idx + 2]) + mn[shift_idx:shift_idx + 1]
    ho_ref[...] = h.astype(ho_ref.dtype)


def _res_ln(x, y, mod_gate, gate_idx, mod_next, shift_idx, ln_w, ln_b, alpha,
            tiles_per_batch, ctx_tiles, h_dtype, x_skip_tiles=0):
    rows, d = y.shape
    ctx_row = mod_gate.shape[0] - 1
    x_tiles = tiles_per_batch + x_skip_tiles
    x_spec = pl.BlockSpec(
        (ROW_TILE, d), lambda r: ((r // tiles_per_batch) * x_tiles + x_skip_tiles + r % tiles_per_batch, 0))
    midx = functools.partial(_mod_index, tiles_per_batch=tiles_per_batch,
                             ctx_tiles=ctx_tiles, ctx_row=ctx_row)
    row_spec = pl.BlockSpec((ROW_TILE, d), lambda r: (r, 0))
    mod_spec = pl.BlockSpec((1, 6, d), lambda r: (midx(r), 0, 0))
    vec_spec = pl.BlockSpec((1, d), lambda r: (0, 0))
    return pl.pallas_call(
        functools.partial(_res_ln_kernel, alpha=alpha, gate_idx=gate_idx, shift_idx=shift_idx),
        out_shape=(jax.ShapeDtypeStruct((rows, d), F32), jax.ShapeDtypeStruct((rows, d), h_dtype)),
        grid=(rows // ROW_TILE,),
        in_specs=[x_spec, row_spec, mod_spec, mod_spec, vec_spec, vec_spec],
        out_specs=(row_spec, row_spec),
        compiler_params=_params("parallel"),
        name="residual_layer_norm",
    )(x, y, mod_gate, mod_next, ln_w.reshape(1, d), ln_b.reshape(1, d))


def _mm_kernel(a_ref, w_ref, o_ref, wbf_ref):
    @pl.when(pl.program_id(1) == 0)
    def _():
        wbf_ref[...] = w_ref[...].astype(BF16)

    o_ref[...] = _dot(a_ref[...], wbf_ref[...]).astype(o_ref.dtype)


def _matmul(a, w, *, tm, tn, out_dtype, col_start=0, n_cols=None):
    m, k = a.shape
    tm = _row_tile(m, tm)
    n_cols = w.shape[1] - col_start if n_cols is None else n_cols
    assert m % tm == 0 and n_cols % tn == 0 and col_start % tn == 0
    off = col_start // tn
    return pl.pallas_call(
        _mm_kernel,
        out_shape=jax.ShapeDtypeStruct((m, n_cols), out_dtype),
        grid=(n_cols // tn, m // tm),
        in_specs=[pl.BlockSpec((tm, k), lambda j, i: (i, 0)),
                  pl.BlockSpec((k, tn), lambda j, i: (0, j + off))],
        out_specs=pl.BlockSpec((tm, tn), lambda j, i: (i, j)),
        scratch_shapes=[pltpu.VMEM((k, tn), BF16)],
        compiler_params=_params("parallel", "arbitrary"),
        name="matmul",
    )(a, w)


def _swiglu_up_kernel(a_ref, wg_ref, wu_ref, o_ref, wgb_ref, wub_ref):
    @pl.when(pl.program_id(1) == 0)
    def _():
        wgb_ref[...] = wg_ref[...].astype(BF16)
        wub_ref[...] = wu_ref[...].astype(BF16)

    a = a_ref[...]
    g = _dot(a, wgb_ref[...])
    u = _dot(a, wub_ref[...])
    o_ref[...] = (_silu(g) * u).astype(o_ref.dtype)


def _swiglu_up(a, w_gu, *, tm, tn):
    m, k = a.shape
    tm = _row_tile(m, tm)
    ff = w_gu.shape[1] // 2
    assert m % tm == 0 and ff % tn == 0
    nj = ff // tn
    return pl.pallas_call(
        _swiglu_up_kernel,
        out_shape=jax.ShapeDtypeStruct((m, ff), BF16),
        grid=(nj, m // tm),
        in_specs=[pl.BlockSpec((tm, k), lambda j, i: (i, 0)),
                  pl.BlockSpec((k, tn), lambda j, i: (0, j)),
                  pl.BlockSpec((k, tn), lambda j, i: (0, j + nj))],
        out_specs=pl.BlockSpec((tm, tn), lambda j, i: (i, j)),
        scratch_shapes=[pltpu.VMEM((k, tn), BF16), pltpu.VMEM((k, tn), BF16)],
        compiler_params=_params("parallel", "arbitrary"),
        name="swiglu_up",
    )(a, w_gu, w_gu)


def _gdn_conv_kernel(cur_ref, prev_ref, next_ref, w_ref, o_ref, xe_ref, *, n_qk_blocks, q_scale):
    i = pl.program_id(1)
    j = pl.program_id(2)
    n_tiles = pl.num_programs(1)
    prev_ok = (i >= 2).astype(F32)
    next_ok = jnp.logical_and(i >= 1, i < n_tiles - 1).astype(F32)
    xe_ref[0:8, :] = prev_ref[0].astype(F32)[8:16] * prev_ok
    xe_ref[8:8 + ROW_TILE, :] = cur_ref[0].astype(F32)
    xe_ref[8 + ROW_TILE:16 + ROW_TILE, :] = next_ref[0].astype(F32)[0:8] * next_ok
    pad = (GDN_CONV - 1) // 2
    is_qk = j < n_qk_blocks
    scale = jnp.where(j < n_qk_blocks // 2, q_scale, 1.0)
    for g in range(xe_ref.shape[1] // GDN_DK):
        sl = slice(g * GDN_DK, (g + 1) * GDN_DK)
        xe = xe_ref[:, sl]
        acc = None
        for t in range(GDN_CONV):
            shifted = xe if t == pad else pltpu.roll(xe, (pad - t) % xe.shape[0], axis=0)
            term = shifted[8:8 + ROW_TILE] * w_ref[t:t + 1, sl]
            acc = term if acc is None else acc + term
        y = _silu(acc)
        inv = lax.rsqrt(jnp.sum(y * y, axis=-1, keepdims=True) + 1e-6) * scale
        o_ref[0, :, sl] = (y * jnp.where(is_qk, inv, 1.0)).astype(o_ref.dtype)


def _gdn_conv(pqz, conv_w, batch):
    rows, _ = pqz.shape
    t_all = rows // batch
    x = pqz.reshape(batch, t_all, pqz.shape[1])
    cb = 1024
    n_tiles = t_all // ROW_TILE
    halo = 16
    last_halo = t_all // halo - 1
    per_tile = ROW_TILE // halo
    return pl.pallas_call(
        functools.partial(_gdn_conv_kernel, n_qk_blocks=2 * GDN_Q // cb, q_scale=GDN_DK ** -0.5),
        out_shape=jax.ShapeDtypeStruct((batch, t_all, GDN_QKV), BF16),
        grid=(batch, n_tiles, GDN_QKV // cb),
        in_specs=[pl.BlockSpec((1, ROW_TILE, cb), lambda b, i, j: (b, i, j)),
                  pl.BlockSpec((1, halo, cb), lambda b, i, j: (b, jnp.maximum(i * per_tile - 1, 0), j)),
                  pl.BlockSpec((1, halo, cb),
                               lambda b, i, j: (b, jnp.minimum((i + 1) * per_tile, last_halo), j)),
                  pl.BlockSpec((GDN_CONV, cb), lambda b, i, j: (0, j))],
        out_specs=pl.BlockSpec((1, ROW_TILE, cb), lambda b, i, j: (b, i, j)),
        scratch_shapes=[pltpu.VMEM((ROW_TILE + 16, cb), F32)],
        compiler_params=_params("parallel", "parallel", "parallel"),
        name="gdn_conv",
    )(x, x, x, conv_w)


def _gdn_gate_kernel(ab_ref, alog_ref, dtb_ref, o_ref):
    ab = ab_ref[...]
    half = LANES // 2
    quarter = LANES // 4
    lane = lax.broadcasted_iota(jnp.int32, ab.shape, 1)
    beta = jax.nn.sigmoid(ab)
    x = ab + dtb_ref[...]
    softplus = jnp.maximum(x, 0.0) + jnp.log1p(jnp.exp(-jnp.abs(x)))
    g = jnp.where(lane >= half, -jnp.exp(alog_ref[...]) * softplus, 0.0)
    g = pltpu.roll(g, half, axis=1)
    n = ab.shape[0]
    ri = lax.broadcasted_iota(jnp.int32, (n, n), 0)
    ci = lax.broadcasted_iota(jnp.int32, (n, n), 1)
    same = (ri // GDN_CHUNK) == (ci // GDN_CHUNK)
    hi = lax.Precision.HIGHEST
    m_fwd = jnp.logical_and(same, ci <= ri).astype(F32)
    m_bwd = jnp.logical_and(same, ci >= ri).astype(F32)
    gc_f = jnp.dot(m_fwd, g, preferred_element_type=F32, precision=hi)
    gc_b = jnp.dot(m_bwd, g, preferred_element_type=F32, precision=hi)
    gc = jnp.where(lane < quarter, gc_f, gc_b)
    for d in range(2):
        parts = [pltpu.roll(val, (q * quarter - d * quarter) % LANES, axis=1)
                 for q, val in enumerate((beta, gc))]
        o_ref[d] = jnp.where(lane < quarter, parts[0], jnp.where(lane < 2 * quarter, parts[1], 0.0))


def _gdn_gates(ab, a_log, dt_bias):
    rows = ab.shape[0]
    zeros = jnp.zeros((LANES // 2,), F32)
    alog = jnp.concatenate([zeros, a_log.reshape(-1).astype(F32)]).reshape(1, LANES)
    dtb = jnp.concatenate([zeros, dt_bias.reshape(-1).astype(F32)]).reshape(1, LANES)
    return pl.pallas_call(
        _gdn_gate_kernel,
        out_shape=jax.ShapeDtypeStruct((2, rows, LANES), F32),
        grid=(rows // ROW_TILE,),
        in_specs=[pl.BlockSpec((ROW_TILE, LANES), lambda r: (r, 0)),
                  pl.BlockSpec((1, LANES), lambda r: (0, 0)),
                  pl.BlockSpec((1, LANES), lambda r: (0, 0))],
        out_specs=pl.BlockSpec((2, ROW_TILE, LANES), lambda r: (0, r, 0)),
        compiler_params=_params("parallel"),
        name="gdn_gates",
    )(ab, alog, dtb)


def _block_masks(ri, ci, n):
    masks = [(ri >> 1) == (ci >> 1)]
    for level in range(1, int(math.log2(n))):
        masks.append(jnp.logical_and((ri >> (level + 1)) == (ci >> (level + 1)),
                                     (ri >> level) != (ci >> level)))
    return masks


def _bmm(a, b):
    return jnp.einsum('hij,hjk->hik', a, b, preferred_element_type=F32)


def _bmm_nt(a, b):
    return jnp.einsum('hik,hjk->hij', a, b, preferred_element_type=F32)


def _bmm_tn(a, b):
    return jnp.einsum('hki,hkj->hij', a, b, preferred_element_type=F32)


def _pair_block_diag(x, left):
    return jnp.concatenate([jnp.where(left, x, 0.0), jnp.where(left, 0.0, x)], axis=1).astype(BF16)


def _unit_triangular_inverse(a, eye, masks, left):
    d = eye - jnp.where(masks[0], a, 0.0)
    for mask in masks[1:]:
        ad = _bmm(jnp.where(mask, a, 0.0).astype(BF16), _pair_block_diag(d, left))
        d = d - _bmm(d.astype(BF16), _pair_block_diag(ad, left))
    return d


def _gdn_scan_kernel(q_ref, k_ref, v_ref, g_ref, gr_ref, o_ref, s_ref):
    d = pl.program_id(1)
    s = pl.program_id(2)

    @pl.when(s == 0)
    def _():
        s_ref[...] = jnp.zeros_like(s_ref)

    c = GDN_CHUNK
    n_h = GDN_V_HEADS
    rep = GDN_V_HEADS // GDN_QK_HEADS
    assert rep == 2
    rev = d == 1
    ri = lax.broadcasted_iota(jnp.int32, (c, 2 * c), 0)
    lane = lax.broadcasted_iota(jnp.int32, (c, 2 * c), 1)
    left = lane < c
    ci = lane % c
    later = jnp.where(rev, ci, ri)
    earlier = jnp.where(rev, ri, ci)
    incl = later >= earlier
    strict = later > earlier
    eye = (ri == ci).astype(F32)
    masks = _block_masks(ri, ci, c)

    def split_pairs(x):
        return jnp.stack([x[j, i * c:(i + 1) * c] for j in range(GDN_QK_HEADS) for i in range(rep)])

    def prepare(cc):
        off = pl.multiple_of(cc * c, c)
        q_all = q_ref[0, pl.ds(off, c), :]
        k_all = k_ref[0, pl.ds(off, c), :]
        v_all = v_ref[0, pl.ds(off, c), :]
        gates = g_ref[0, pl.ds(off, c), :]
        gc_rows = gr_ref[0, 0, cc]
        qs = [q_all[:, i * GDN_DK:(i + 1) * GDN_DK] for i in range(GDN_QK_HEADS)]
        ks = [k_all[:, i * GDN_DK:(i + 1) * GDN_DK] for i in range(GDN_QK_HEADS)]
        q_over_k = jnp.stack([jnp.concatenate([qs[i], ks[i]], axis=0) for i in range(GDN_QK_HEADS)])
        k_twice = jnp.stack([jnp.concatenate([ks[i], ks[i]], axis=0) for i in range(GDN_QK_HEADS)])
        qk_kk = _bmm_nt(q_over_k, k_twice)

        def col(quantity, h):
            idx = quantity * n_h + h
            return gates[:, idx:idx + 1]

        a_l, qk_l, rhs_l, qg_l, kt_l, egl_l = [], [], [], [], [], []
        for j in range(GDN_QK_HEADS):
            kf = ks[j].astype(F32)
            qf = qs[j].astype(F32)
            rhs_pair, beta_pair, gc_pair = [], [], []
            for h in (rep * j, rep * j + 1):
                beta = jnp.broadcast_to(col(0, h), (c, 2 * c))
                gc = jnp.broadcast_to(col(1, h), (c, 2 * c))
                g_total = jnp.where(rev, gc[0:1, :], gc[c - 1:c, :])
                eg = jnp.exp(gc)
                ek = jnp.exp(g_total - gc)
                vf = v_all[:, h * GDN_DV:(h + 1) * GDN_DV].astype(F32)
                rhs_pair.append(jnp.concatenate([vf * beta, kf * (beta * eg)], axis=1).astype(BF16))
                qg_l.append((qf * eg).astype(BF16))
                kt_l.append((kf * ek).T.astype(BF16))
                egl_l.append(jnp.exp(g_total))
                beta_pair.append(beta)
                gc_pair.append(gc)
            rhs_l.append(jnp.concatenate(rhs_pair, axis=0))
            beta_p = jnp.where(left, beta_pair[0], beta_pair[1])
            gc_p = jnp.where(left, gc_pair[0], gc_pair[1])
            gdiff = gc_p - gc_rows[j:j + 1, :]
            decay = jnp.where(incl, jnp.exp(jnp.where(incl, gdiff, 0.0)), 0.0)
            a_l.append(jnp.where(strict, qk_kk[j, c:] * (beta_p * decay), 0.0))
            qk_l.append(qk_kk[j, 0:c] * decay)
        return a_l, qk_l, rhs_l, qg_l, kt_l, egl_l

    def chunk_step(position, carry):
        cc = jnp.where(rev, CHUNKS_PER_TILE - 1 - position, position)
        a_v, qk_v, rhs_v, qg_v, kt_v, egl_v = (jnp.stack(parts) for parts in prepare(cc))
        t_inv = _unit_triangular_inverse(a_v, eye, masks, left)
        uw = split_pairs(_bmm(_pair_block_diag(t_inv, left), rhs_v))
        u = uw[:, :, 0:GDN_DV]
        w = uw[:, :, GDN_DV:]
        state = s_ref[...]
        state_b = state.astype(BF16)
        w_qg = _bmm(jnp.concatenate([w.astype(BF16), qg_v], axis=1), state_b)
        v_new = u - w_qg[:, 0:c]
        v_new_b = v_new.astype(BF16)
        v_pairs = jnp.stack([jnp.concatenate([v_new_b[rep * j], v_new_b[rep * j + 1]], axis=0)
                             for j in range(GDN_QK_HEADS)])
        o = w_qg[:, c:] + split_pairs(_bmm(_pair_block_diag(qk_v, left), v_pairs))
        s_ref[...] = state * egl_v + _bmm(kt_v, v_new_b)
        off = pl.multiple_of(cc * c, c)
        for h in range(n_h):
            o_ref[0, 0, pl.ds(off, c), h * GDN_DV:(h + 1) * GDN_DV] = o[h].astype(o_ref.dtype)
        return carry

    lax.fori_loop(0, CHUNKS_PER_TILE, chunk_step, 0, unroll=2)


def _gdn_scan(qkv, gates, gc_rows, batch):
    _, t_all, _ = qkv.shape
    n_tiles = t_all // ROW_TILE

    def tile(d, s):
        return jnp.where(d == 0, s, jnp.where(s == 0, 0, n_tiles - s))

    return pl.pallas_call(
        _gdn_scan_kernel,
        out_shape=jax.ShapeDtypeStruct((2, batch, t_all, GDN_V), BF16),
        grid=(batch, 2, n_tiles),
        in_specs=[
            pl.BlockSpec((1, ROW_TILE, GDN_Q), lambda b, d, s: (b, tile(d, s), 0)),
            pl.BlockSpec((1, ROW_TILE, GDN_Q), lambda b, d, s: (b, tile(d, s), 1)),
            pl.BlockSpec((1, ROW_TILE, GDN_V), lambda b, d, s: (b, tile(d, s), 1)),
            pl.BlockSpec((1, ROW_TILE, LANES), lambda b, d, s: (d, b * n_tiles + tile(d, s), 0)),
            pl.BlockSpec((1, 1, CHUNKS_PER_TILE, GDN_QK_HEADS, 2 * GDN_CHUNK),
                         lambda b, d, s: (b, d, tile(d, s), 0, 0)),
        ],
        out_specs=pl.BlockSpec((1, 1, ROW_TILE, GDN_V), lambda b, d, s: (d, b, tile(d, s), 0)),
        scratch_shapes=[pltpu.VMEM((GDN_V_HEADS, GDN_DK, GDN_DV), F32)],
        compiler_params=_params("parallel", "arbitrary", "arbitrary"),
        name="gdn_scan",
    )(qkv, qkv, qkv, gates, gc_rows)


def _gdn_norm_gate_kernel(of_ref, ob_ref, z_ref, w_ref, o_ref):
    w = w_ref[...]
    for g in range(o_ref.shape[1] // GDN_DV):
        sl = slice(g * GDN_DV, (g + 1) * GDN_DV)
        og = of_ref[0, :, sl].astype(F32) + ob_ref[0, :, sl].astype(F32)
        inv = lax.rsqrt(jnp.mean(og * og, axis=-1, keepdims=True) + RMS_EPS)
        o_ref[:, sl] = (og * inv * w * _silu(z_ref[:, sl].astype(F32))).astype(o_ref.dtype)


def _gdn_norm_gate(o2, pqz, norm_w):
    _, rows, width = o2.shape
    cb = width
    z_off = GDN_QKV // cb
    return pl.pallas_call(
        _gdn_norm_gate_kernel,
        out_shape=jax.ShapeDtypeStruct((rows, width), BF16),
        grid=(rows // ROW_TILE, width // cb),
        in_specs=[pl.BlockSpec((1, ROW_TILE, cb), lambda r, j: (0, r, j)),
                  pl.BlockSpec((1, ROW_TILE, cb), lambda r, j: (1, r, j)),
                  pl.BlockSpec((ROW_TILE, cb), lambda r, j: (r, z_off + j)),
                  pl.BlockSpec((1, GDN_DV), lambda r, j: (0, 0))],
        out_specs=pl.BlockSpec((ROW_TILE, cb), lambda r, j: (r, j)),
        compiler_params=_params("parallel", "parallel"),
        name="gdn_norm_gate",
    )(o2, o2, pqz, norm_w.reshape(1, GDN_DV))


def _gdn_mixer(h, batch, w_in, conv_w, a_log, dt_bias, norm_w, w_out):
    rows = h.shape[0]
    t_all = rows // batch
    pqz = _matmul(h, w_in, tm=1024, tn=1024, out_dtype=BF16, col_start=0, n_cols=GDN_QKV + GDN_V)
    ab = _matmul(h, w_in, tm=1024, tn=LANES, out_dtype=F32, col_start=GDN_QKV + GDN_V, n_cols=LANES)
    gates = _gdn_gates(ab, a_log, dt_bias)
    n_chunks = t_all // GDN_CHUNK
    gc_rows = gates[:, :, GDN_V_HEADS:2 * GDN_V_HEADS]
    gc_rows = gc_rows.reshape(2, batch, n_chunks, GDN_CHUNK, GDN_V_HEADS).transpose(1, 0, 2, 4, 3)
    gc_rows = gc_rows.reshape(batch, 2, n_chunks, GDN_QK_HEADS, 2 * GDN_CHUNK)
    qkv = _gdn_conv(pqz, conv_w, batch)
    o2 = _gdn_scan(qkv, gates, gc_rows, batch)
    og = _gdn_norm_gate(o2.reshape(2, rows, GDN_V), pqz, norm_w)
    return _matmul(og, w_out, tm=1024, tn=512, out_dtype=BF16)


def _rope_tables(n_ctx, n_lat):
    rows = n_lat // GRID_W
    row = jnp.repeat(jnp.arange(rows, dtype=F32), GRID_W)
    col = jnp.tile(jnp.arange(GRID_W, dtype=F32), rows)
    inv_freq = 1.0 / (ROPE_BASE ** (jnp.arange(0, ROPE_AXIS_DIM, 2, dtype=F32) / ROPE_AXIS_DIM))
    ang_r = row[:, None] * inv_freq
    ang_c = col[:, None] * inv_freq
    cos = jnp.concatenate([jnp.cos(ang_r), jnp.cos(ang_r), jnp.cos(ang_c), jnp.cos(ang_c)], axis=1)
    sin = jnp.concatenate([-jnp.sin(ang_r), jnp.sin(ang_r), -jnp.sin(ang_c), jnp.sin(ang_c)], axis=1)
    cos = jnp.concatenate([jnp.ones((n_ctx, DIFF_DH), F32), cos], axis=0)
    sin = jnp.concatenate([jnp.zeros((n_ctx, DIFF_DH), F32), sin], axis=0)
    return cos, sin


def _qkv_rope_kernel(a_ref, w_ref, cos_ref, sin_ref, o_ref, wbf_ref, *, n_rope_blocks, n_q_blocks,
                     q_scale):
    j = pl.program_id(0)

    @pl.when(pl.program_id(1) == 0)
    def _():
        wbf_ref[...] = w_ref[...].astype(BF16)

    acc = _dot(a_ref[...], wbf_ref[...])

    @pl.when(j >= n_rope_blocks)
    def _():
        o_ref[...] = acc.astype(o_ref.dtype)

    @pl.when(j < n_rope_blocks)
    def _():
        scale = jnp.where(j < n_q_blocks, q_scale, 1.0)
        cos = cos_ref[...] * scale
        sin = sin_ref[...] * scale
        quarter = ROPE_AXIS_DIM // 2
        lane = lax.broadcasted_iota(jnp.int32, cos.shape, 1)
        first = (lane % ROPE_AXIS_DIM) < quarter
        for g in range(acc.shape[1] // DIFF_DH):
            sl = slice(g * DIFF_DH, (g + 1) * DIFF_DH)
            xg = acc[:, sl]
            partner = jnp.where(first, pltpu.roll(xg, DIFF_DH - quarter, axis=1),
                                pltpu.roll(xg, quarter, axis=1))
            o_ref[:, sl] = (xg * cos + partner * sin).astype(o_ref.dtype)


def _qkv_rope(a, w, cos, sin, *, tm, tn, q_scale):
    m, k = a.shape
    tm = _row_tile(m, tm)
    n = w.shape[1]
    assert n % tn == 0 and DIFF_Q % tn == 0
    return pl.pallas_call(
        functools.partial(_qkv_rope_kernel, n_rope_blocks=2 * DIFF_Q // tn, n_q_blocks=DIFF_Q // tn,
                          q_scale=q_scale),
        out_shape=jax.ShapeDtypeStruct((m, n), BF16),
        grid=(n // tn, m // tm),
        in_specs=[pl.BlockSpec((tm, k), lambda j, i: (i, 0)),
                  pl.BlockSpec((k, tn), lambda j, i: (0, j)),
                  pl.BlockSpec((tm, DIFF_DH), lambda j, i: (i, 0)),
                  pl.BlockSpec((tm, DIFF_DH), lambda j, i: (i, 0))],
        out_specs=pl.BlockSpec((tm, tn), lambda j, i: (i, j)),
        scratch_shapes=[pltpu.VMEM((k, tn), BF16)],
        compiler_params=_params("parallel", "arbitrary"),
        name="qkv_rope",
    )(a, w, cos, sin)


def _diff_attn_kernel(lam_ref, nw_ref, q_ref, k_ref, v_ref, o_ref, *, lam_init, key_chunk):
    lp = lam_ref[...]
    lam = (jnp.exp(jnp.sum(lp[0:1] * lp[1:2], axis=-1, keepdims=True))
           - jnp.exp(jnp.sum(lp[2:3] * lp[3:4], axis=-1, keepdims=True)) + lam_init)
    q = q_ref[0]
    n_keys = k_ref.shape[1]

    def attend(m):
        sl = slice(m * DIFF_DH, (m + 1) * DIFF_DH)
        qm = q[:, sl]
        m_run = l_run = acc = None
        for c in range(pl.cdiv(n_keys, key_chunk)):
            rows = slice(c * key_chunk, min((c + 1) * key_chunk, n_keys))
            s2 = _dot_nt(qm, k_ref[0, rows, sl])
            m_c = jnp.max(s2, axis=-1, keepdims=True)
            if c == 0:
                m_new = m_c
                p = jnp.exp2(s2 - m_new)
                l_run = jnp.sum(p, axis=-1, keepdims=True)
                acc = _dot(p.astype(BF16), v_ref[0, rows, :])
            else:
                m_new = jnp.maximum(m_run, m_c)
                rescale = jnp.exp2(m_run - m_new)
                p = jnp.exp2(s2 - m_new)
                l_run = l_run * rescale + jnp.sum(p, axis=-1, keepdims=True)
                acc = acc * rescale + _dot(p.astype(BF16), v_ref[0, rows, :])
            m_run = m_new
        return acc, l_run

    o1, l1 = attend(0)
    o2, l2 = attend(1)
    o = o1 * (1.0 / l1) - o2 * (lam / l2)
    inv = lax.rsqrt(jnp.mean(o * o, axis=-1, keepdims=True) + RMS_EPS)
    o_ref[0] = (o * inv * nw_ref[...] * (1.0 - lam_init)).astype(o_ref.dtype)


def _diff_attention(qkv, batch, n_ctx, lam_p, norm_w, lam_init):
    t_all = qkv.shape[0] // batch
    n_lat = t_all - n_ctx
    tq = ROW_TILE
    assert n_ctx % tq == 0
    ctx_tiles = n_ctx // tq
    v3 = qkv.reshape(batch, t_all, qkv.shape[1])
    k_off = DIFF_Q // DIFF_DV
    v_off = 2 * DIFF_Q // DIFF_DV
    return pl.pallas_call(
        functools.partial(_diff_attn_kernel, lam_init=lam_init, key_chunk=2 * ROW_TILE),
        out_shape=jax.ShapeDtypeStruct((batch, n_lat, DIFF_HEADS * DIFF_DV), BF16),
        grid=(batch, DIFF_HEADS, n_lat // tq),
        in_specs=[pl.BlockSpec((4, DIFF_DH), lambda b, h, i: (0, 0)),
                  pl.BlockSpec((1, DIFF_DV), lambda b, h, i: (0, 0)),
                  pl.BlockSpec((1, tq, DIFF_DV), lambda b, h, i: (b, i + ctx_tiles, h)),
                  pl.BlockSpec((1, t_all, DIFF_DV), lambda b, h, i: (b, 0, k_off + h)),
                  pl.BlockSpec((1, t_all, DIFF_DV), lambda b, h, i: (b, 0, v_off + h))],
        out_specs=pl.BlockSpec((1, tq, DIFF_DV), lambda b, h, i: (b, i, h)),
        compiler_params=_params("parallel", "parallel", "arbitrary"),
        name="diff_attention",
    )(lam_p.astype(F32), norm_w.reshape(1, DIFF_DV), v3, v3, v3)


def _router_kernel(h_ref, r_ref, idx_ref, w_ref):
    logits = _dot_nt(r_ref[...].astype(BF16), h_ref[...].astype(BF16))
    e = lax.broadcasted_iota(jnp.int32, logits.shape, 0)
    n_e = logits.shape[0]
    m1 = jnp.max(logits, axis=0, keepdims=True)
    i1 = jnp.min(jnp.where(logits == m1, e, n_e), axis=0, keepdims=True)
    rest = jnp.where(e == i1, -jnp.inf, logits)
    m2 = jnp.max(rest, axis=0, keepdims=True)
    i2 = jnp.min(jnp.where(rest == m2, e, n_e), axis=0, keepdims=True)
    t = jnp.exp(m2 - m1)
    idx_ref[0:1, :] = i1
    idx_ref[1:2, :] = i2
    w_ref[0:1, :] = 1.0 / (1.0 + t)
    w_ref[1:2, :] = t / (1.0 + t)


def _router(h, router):
    rows, d = h.shape
    tm = _row_tile(rows, 1024)
    return pl.pallas_call(
        _router_kernel,
        out_shape=(jax.ShapeDtypeStruct((2, rows), jnp.int32), jax.ShapeDtypeStruct((2, rows), F32)),
        grid=(rows // tm,),
        in_specs=[pl.BlockSpec((tm, d), lambda i: (i, 0)),
                  pl.BlockSpec((N_EXPERTS, d), lambda i: (0, 0))],
        out_specs=(pl.BlockSpec((2, tm), lambda i: (0, i)), pl.BlockSpec((2, tm), lambda i: (0, i))),
        compiler_params=_params("parallel"),
        name="moe_router",
    )(h, router.T)


def _row_copy(src_hbm, dst_vmem, sem, src_row, dst_row):
    return pltpu.make_async_copy(src_hbm.at[pl.ds(src_row, 1)], dst_vmem.at[pl.ds(dst_row, 1)], sem)


def _gather_rows_kernel(idx_ref, src_ref, o_ref, buf_ref, sem):
    n = o_ref.shape[0]
    i = pl.program_id(0)
    slot = i % 2

    def issue_tile(tile, dst_slot):
        def body(r2, carry):
            for p in range(2):
                r = 2 * r2 + p
                _row_copy(src_ref, buf_ref.at[dst_slot], sem.at[dst_slot],
                          idx_ref[tile * n + r], r).start(priority=p)
            return carry
        lax.fori_loop(0, n // 2, body, 0, unroll=4)

    @pl.when(i == 0)
    def _():
        issue_tile(0, 0)

    @pl.when(i + 1 < pl.num_programs(0))
    def _():
        issue_tile(i + 1, 1 - slot)

    def wait_row(r, carry):
        _row_copy(src_ref, buf_ref.at[slot], sem.at[slot], 0, r).wait()
        return carry

    lax.fori_loop(0, n, wait_row, 0, unroll=8)
    o_ref[...] = buf_ref[slot].astype(o_ref.dtype)


def _gather_rows(src, idx, out_dtype):
    n = idx.shape[0]
    d = src.shape[1]
    tg = ROW_TILE
    return pl.pallas_call(
        _gather_rows_kernel,
        out_shape=jax.ShapeDtypeStruct((n, d), out_dtype),
        grid_spec=pltpu.PrefetchScalarGridSpec(
            num_scalar_prefetch=1,
            grid=(n // tg,),
            in_specs=[pl.BlockSpec(memory_space=pl.ANY)],
            out_specs=pl.BlockSpec((tg, d), lambda i, idx: (i, 0)),
            scratch_shapes=[pltpu.VMEM((2, tg, d), src.dtype), pltpu.SemaphoreType.DMA((2,))]),
        compiler_params=_params("arbitrary"),
        name="moe_gather",
    )(idx, src)


def _expert_up_kernel(te_ref, nt_ref, a_ref, wg_ref, wu_ref, o_ref, wgb_ref, wub_ref):
    i = pl.program_id(1)
    new_expert = jnp.logical_or(i == 0, te_ref[i] != te_ref[jnp.maximum(i - 1, 0)])

    @pl.when(new_expert)
    def _():
        wgb_ref[...] = wg_ref[0].astype(BF16)
        wub_ref[...] = wu_ref[0].astype(BF16)

    @pl.when(i < nt_ref[0])
    def _():
        a = a_ref[...]
        g = _dot(a, wgb_ref[...])
        u = _dot(a, wub_ref[...])
        o_ref[...] = (_silu(g) * u).astype(o_ref.dtype)

    @pl.when(i >= nt_ref[0])
    def _():
        o_ref[...] = jnp.zeros_like(o_ref)


def _expert_up(xs, tile_expert, n_tiles_used, w_gu, *, tn):
    p, k = xs.shape
    ff = w_gu.shape[2] // 2
    nj = ff // tn
    tm = MOE_TILE
    return pl.pallas_call(
        _expert_up_kernel,
        out_shape=jax.ShapeDtypeStruct((p, ff), BF16),
        grid_spec=pltpu.PrefetchScalarGridSpec(
            num_scalar_prefetch=2,
            grid=(nj, p // tm),
            in_specs=[pl.BlockSpec((tm, k), lambda j, i, te, nt: (i, 0)),
                      pl.BlockSpec((1, k, tn), lambda j, i, te, nt: (te[i], 0, j)),
                      pl.BlockSpec((1, k, tn), lambda j, i, te, nt: (te[i], 0, j + nj))],
            out_specs=pl.BlockSpec((tm, tn), lambda j, i, te, nt: (i, j)),
            scratch_shapes=[pltpu.VMEM((k, tn), BF16), pltpu.VMEM((k, tn), BF16)]),
        compiler_params=_params("parallel", "arbitrary"),
        name="moe_expert_up",
    )(tile_expert, n_tiles_used, xs, w_gu, w_gu)


def _expert_down_kernel(te_ref, nt_ref, a_ref, w_ref, o_ref, wb_ref):
    i = pl.program_id(1)
    new_expert = jnp.logical_or(i == 0, te_ref[i] != te_ref[jnp.maximum(i - 1, 0)])

    @pl.when(new_expert)
    def _():
        wb_ref[...] = w_ref[0].astype(BF16)

    @pl.when(i < nt_ref[0])
    def _():
        o_ref[...] = _dot(a_ref[...], wb_ref[...]).astype(o_ref.dtype)

    @pl.when(i >= nt_ref[0])
    def _():
        o_ref[...] = jnp.zeros_like(o_ref)


def _expert_down(act, tile_expert, n_tiles_used, w_down, *, tn):
    p, k = act.shape
    n = w_down.shape[2]
    tm = MOE_TILE
    return pl.pallas_call(
        _expert_down_kernel,
        out_shape=jax.ShapeDtypeStruct((p, n), F32),
        grid_spec=pltpu.PrefetchScalarGridSpec(
            num_scalar_prefetch=2,
            grid=(n // tn, p // tm),
            in_specs=[pl.BlockSpec((tm, k), lambda j, i, te, nt: (i, 0)),
                      pl.BlockSpec((1, k, tn), lambda j, i, te, nt: (te[i], 0, j))],
            out_specs=pl.BlockSpec((tm, tn), lambda j, i, te, nt: (i, j)),
            scratch_shapes=[pltpu.VMEM((k, tn), BF16)]),
        compiler_params=_params("parallel", "arbitrary"),
        name="moe_expert_down",
    )(tile_expert, n_tiles_used, act, w_down)


def _combine_ln_kernel(slot_ref, ys_ref, w_ref, x_ref, mg_ref, lnw_ref, lnb_ref, o_ref,
                       buf_ref, sem, *, alpha, gate_idx):
    n = x_ref.shape[0]
    n_tok = n * pl.num_programs(0)
    i = pl.program_id(0)
    slot = i % 2

    def issue_tile(tile, dst_slot):
        def body(r, carry):
            for choice in range(2):
                _row_copy(ys_ref, buf_ref.at[dst_slot, choice], sem.at[dst_slot],
                          slot_ref[choice * n_tok + tile * n + r], r).start(priority=choice)
            return carry
        lax.fori_loop(0, n, body, 0, unroll=4)

    @pl.when(i == 0)
    def _():
        issue_tile(0, 0)

    @pl.when(i + 1 < pl.num_programs(0))
    def _():
        issue_tile(i + 1, 1 - slot)

    def wait_row(r, carry):
        for choice in range(2):
            _row_copy(ys_ref, buf_ref.at[slot, choice], sem.at[slot], 0, r).wait()
        return carry

    lax.fori_loop(0, n, wait_row, 0, unroll=4)
    w = w_ref[...]
    f = w[:, 0:1] * buf_ref[slot, 0] + w[:, 1:2] * buf_ref[slot, 1]
    gate = mg_ref[0][gate_idx:gate_idx + 1]
    t = alpha * x_ref[...] + gate * f
    o_ref[...] = _layer_norm_rows(t, lnw_ref[...], lnb_ref[...])


def _combine_ln(ys, slots, weights, x, mod_gate, gate_idx, ln_w, ln_b, alpha, tiles_per_batch):
    rows, d = x.shape
    tc = ROW_TILE
    row_spec = pl.BlockSpec((tc, d), lambda r, s: (r, 0))
    vec_spec = pl.BlockSpec((1, d), lambda r, s: (0, 0))
    return pl.pallas_call(
        functools.partial(_combine_ln_kernel, alpha=alpha, gate_idx=gate_idx),
        out_shape=jax.ShapeDtypeStruct((rows, d), F32),
        grid_spec=pltpu.PrefetchScalarGridSpec(
            num_scalar_prefetch=1,
            grid=(rows // tc,),
            in_specs=[pl.BlockSpec(memory_space=pl.ANY),
                      pl.BlockSpec((tc, 2), lambda r, s: (r, 0)),
                      row_spec,
                      pl.BlockSpec((1, 6, d), lambda r, s: (r // tiles_per_batch, 0, 0)),
                      vec_spec, vec_spec],
            out_specs=row_spec,
            scratch_shapes=[pltpu.VMEM((2, 2, tc, d), F32), pltpu.SemaphoreType.DMA((2,))]),
        compiler_params=_params("arbitrary"),
        name="moe_combine_layer_norm",
    )(slots, ys, weights, x, mod_gate, ln_w.reshape(1, d), ln_b.reshape(1, d))


def _routing_tables(top_i, n_slots):
    n_tok = top_i.shape[1]
    tm = MOE_TILE
    pair_e = top_i.T.reshape(-1)
    onehot = (pair_e[:, None] == jnp.arange(N_EXPERTS, dtype=jnp.int32)[None, :]).astype(jnp.int32)
    rank = jnp.cumsum(onehot, axis=0) - onehot
    counts = jnp.sum(onehot, axis=0)
    padded = ((counts + tm - 1) // tm) * tm
    ends = jnp.cumsum(padded)
    starts = ends - padded
    slot = jnp.sum(onehot * (starts[None, :] + rank), axis=1)
    token_of_slot = jnp.zeros((n_slots,), jnp.int32).at[slot].set(
        jnp.arange(2 * n_tok, dtype=jnp.int32) // 2)
    tile_start = jnp.arange(n_slots // tm, dtype=jnp.int32) * tm
    tile_expert = jnp.minimum(
        jnp.sum((tile_start[:, None] >= ends[None, :]).astype(jnp.int32), axis=1), N_EXPERTS - 1)
    n_tiles_used = (ends[-1:] // tm).astype(jnp.int32)
    slots = slot.reshape(n_tok, 2).T.reshape(-1)
    return slots, token_of_slot, tile_expert.astype(jnp.int32), n_tiles_used


def _moe(h_f32, router, w_gu, w_down):
    n_tok = h_f32.shape[0]
    top_i, top_w = _router(h_f32, router)
    n_slots = 2 * n_tok + N_EXPERTS * MOE_TILE
    slots, token_of_slot, tile_expert, n_tiles_used = _routing_tables(top_i, n_slots)
    xs = _gather_rows(h_f32, token_of_slot, BF16)
    act = _expert_up(xs, tile_expert, n_tiles_used, w_gu, tn=1024)
    ys = _expert_down(act, tile_expert, n_tiles_used, w_down, tn=512)
    return ys, slots, top_w.T


def kernel(x, c, ctx, c_ctx, ada_w, ada_b, ln_w, ln_b, gdn_w_in, gdn_conv_w, gdn_a_log, gdn_dt_bias,
           gdn_norm_w, gdn_w_out, ffn_w_gu, ffn_w_down, diff_w_in, diff_lambda, diff_norm_w,
           diff_w_out, moe_router, moe_w_gu, moe_w_down):
    batch, n_lat, d = x.shape
    n_ctx = ctx.shape[1]
    depth = ada_w.shape[0]
    assert depth == 2 and n_ctx == ROW_TILE and n_lat % ROW_TILE == 0
    t_all = n_ctx + n_lat
    alpha = (2.0 * depth) ** 0.25
    tiles_all = t_all // ROW_TILE
    tiles_lat = n_lat // ROW_TILE

    n_cond = batch + 1
    cond = jnp.concatenate([c, c_ctx[None, :], jnp.zeros((-n_cond % 8, d), F32)], axis=0)
    mod = _ada_modulation(cond, ada_w, ada_b)[:, :n_cond].reshape(depth, n_cond, 6, d)

    xa = jnp.concatenate([ctx, x], axis=1).reshape(batch * t_all, d)

    h = _modulate(xa, mod[0], tiles_all, 1)
    y = _gdn_mixer(h, batch, gdn_w_in[0], gdn_conv_w[0], gdn_a_log[0], gdn_dt_bias[0],
                   gdn_norm_w[0], gdn_w_out[0])
    xa, h = _res_ln(xa, y, mod[0], 2, mod[0], 3, ln_w[0, 0], ln_b[0, 0], alpha, tiles_all, 1, BF16)
    act = _swiglu_up(h, ffn_w_gu[0], tm=1024, tn=512)
    f = _matmul(act, ffn_w_down[0], tm=512, tn=512, out_dtype=BF16)
    xa, h = _res_ln(xa, f, mod[0], 5, mod[1], 0, ln_w[0, 1], ln_b[0, 1], alpha, tiles_all, 1, BF16)

    lam_init = 0.8 - 0.6 * math.exp(-0.3 * 1)
    cos, sin = _rope_tables(n_ctx, n_lat)
    qkv = _qkv_rope(h, diff_w_in[0], jnp.tile(cos, (batch, 1)), jnp.tile(sin, (batch, 1)),
                    tm=1024, tn=1024, q_scale=DIFF_DH ** -0.5 * math.log2(math.e))
    o = _diff_attention(qkv, batch, n_ctx, diff_lambda[0], diff_norm_w[0], lam_init)
    y = _matmul(o.reshape(batch * n_lat, DIFF_HEADS * DIFF_DV), diff_w_out[0],
                tm=1024, tn=1024, out_dtype=BF16)
    xl, hf = _res_ln(xa, y, mod[1], 2, mod[1], 3, ln_w[1, 0], ln_b[1, 0], alpha, tiles_lat, 0, F32,
                     x_skip_tiles=tiles_all - tiles_lat)
    ys, slots, top_w = _moe(hf, moe_router[0], moe_w_gu[0], moe_w_down[0])
    out = _combine_ln(ys, slots, top_w, xl, mod[1], 5, ln_w[1, 1], ln_b[1, 1], alpha, tiles_lat)
    return out.reshape(batch, n_lat, d)
```

```python
import functools
import math

import jax
import jax.numpy as jnp
from jax import lax
from jax.experimental import pallas as pl
from jax.experimental.pallas import tpu as pltpu

F32 = jnp.float32
BF16 = jnp.bfloat16

GDN_QK_HEADS = 16
GDN_V_HEADS = 32
GDN_DK = 128
GDN_DV = 128
GDN_CONV = 5
GDN_CHUNK = 64
GDN_Q = GDN_QK_HEADS * GDN_DK
GDN_V = GDN_V_HEADS * GDN_DV
GDN_QKV = 2 * GDN_Q + GDN_V
DIFF_HEADS = 8
DIFF_DH = 128
DIFF_DV = 2 * DIFF_DH
DIFF_Q = DIFF_HEADS * 2 * DIFF_DH
GRID_W = 64
ROPE_BASE = 10000.0
ROPE_AXIS_DIM = DIFF_DH // 2
N_EXPERTS = 8
LN_EPS = 1e-5
RMS_EPS = 1e-6

V7X_VMEM_BYTES = 64 * 1024 * 1024
VMEM_LIMIT = V7X_VMEM_BYTES - 8 * 1024 * 1024
LANES = 128
ROW_TILE = 256
CHUNKS_PER_TILE = ROW_TILE // GDN_CHUNK
MOE_TILE = 512


def _params(*sem):
    return pltpu.CompilerParams(dimension_semantics=sem, vmem_limit_bytes=VMEM_LIMIT)


def _row_tile(m, preferred):
    t = preferred
    while m % t and t > ROW_TILE:
        t //= 2
    assert m % t == 0
    return t


def _dot(a, b):
    return jnp.dot(a, b, preferred_element_type=F32)


def _dot_nt(a, b):
    return lax.dot_general(a, b, (((1,), (1,)), ((), ())), preferred_element_type=F32)


def _dot_tn(a, b):
    return lax.dot_general(a, b, (((0,), (0,)), ((), ())), preferred_element_type=F32)


def _silu(x):
    return x * jax.nn.sigmoid(x)


def _ada_kernel(c_ref, w_ref, b_ref, o_ref):
    cond = _silu(c_ref[...])
    o_ref[0] = jnp.dot(cond, w_ref[0], preferred_element_type=F32,
                       precision=lax.Precision.HIGHEST) + b_ref[0]


def _ada_modulation(cond, ada_w, ada_b):
    depth, d, n = ada_w.shape
    rows = cond.shape[0]
    tn = 1024
    return pl.pallas_call(
        _ada_kernel,
        out_shape=jax.ShapeDtypeStruct((depth, rows, n), F32),
        grid=(depth, n // tn),
        in_specs=[pl.BlockSpec((rows, d), lambda l, j: (0, 0)),
                  pl.BlockSpec((1, d, tn), lambda l, j: (l, 0, j)),
                  pl.BlockSpec((1, 1, tn), lambda l, j: (l, 0, j))],
        out_specs=pl.BlockSpec((1, rows, tn), lambda l, j: (l, 0, j)),
        compiler_params=_params("parallel", "parallel"),
        name="ada_modulation",
    )(cond, ada_w, ada_b.reshape(depth, 1, n))


def _mod_index(r, tiles_per_batch, ctx_tiles, ctx_row):
    return jnp.where(r % tiles_per_batch < ctx_tiles, ctx_row, r // tiles_per_batch)


def _token_source_specs(tiles_per_batch, ctx_tiles, d):
    lat_tiles = tiles_per_batch - ctx_tiles

    def lat_index(r):
        w = jnp.clip(r % tiles_per_batch - ctx_tiles, 0, lat_tiles - 1)
        return ((r // tiles_per_batch) * lat_tiles + w, 0)

    def ctx_index(r):
        w = jnp.minimum(r % tiles_per_batch, ctx_tiles - 1)
        return ((r // tiles_per_batch) * ctx_tiles + w, 0)

    return pl.BlockSpec((ROW_TILE, d), lat_index), pl.BlockSpec((ROW_TILE, d), ctx_index)


def _stream_tile(x_ref, c_ref, tiles_per_batch, ctx_tiles):
    is_ctx = pl.program_id(0) % tiles_per_batch < ctx_tiles
    return jnp.where(is_ctx, c_ref[...], x_ref[...])


def _modulate_kernel(x_ref, c_ref, m_ref, o_ref, *, tiles_per_batch, ctx_tiles):
    m = m_ref[0]
    x = _stream_tile(x_ref, c_ref, tiles_per_batch, ctx_tiles)
    o_ref[...] = (x * (1.0 + m[1:2]) + m[0:1]).astype(o_ref.dtype)


def _modulate(x_lat, x_ctx, mod, tiles_per_batch, ctx_tiles):
    d = x_lat.shape[1]
    rows = x_lat.shape[0] + x_ctx.shape[0]
    ctx_row = mod.shape[0] - 1
    midx = functools.partial(_mod_index, tiles_per_batch=tiles_per_batch,
                             ctx_tiles=ctx_tiles, ctx_row=ctx_row)
    lat_spec, ctx_spec = _token_source_specs(tiles_per_batch, ctx_tiles, d)
    return pl.pallas_call(
        functools.partial(_modulate_kernel, tiles_per_batch=tiles_per_batch, ctx_tiles=ctx_tiles),
        out_shape=jax.ShapeDtypeStruct((rows, d), BF16),
        grid=(rows // ROW_TILE,),
        in_specs=[lat_spec, ctx_spec, pl.BlockSpec((1, 6, d), lambda r: (midx(r), 0, 0))],
        out_specs=pl.BlockSpec((ROW_TILE, d), lambda r: (r, 0)),
        compiler_params=_params("parallel"),
        name="modulate",
    )(x_lat, x_ctx, mod)


def _layer_norm_rows(t, w, b):
    mu = jnp.mean(t, axis=-1, keepdims=True)
    tc = t - mu
    var = jnp.mean(tc * tc, axis=-1, keepdims=True)
    return tc * lax.rsqrt(var + LN_EPS) * w + b


def _res_ln_kernel(x_ref, c_ref, y_ref, mg_ref, mn_ref, lnw_ref, lnb_ref, xo_ref, ho_ref,
                   *, alpha, gate_idx, shift_idx, stream):
    gate = mg_ref[0][gate_idx:gate_idx + 1]
    x = _stream_tile(x_ref, c_ref, *stream) if stream else x_ref[...]
    t = alpha * x + gate * y_ref[...].astype(F32)
    xn = _layer_norm_rows(t, lnw_ref[...], lnb_ref[...])
    xo_ref[...] = xn
    mn = mn_ref[0]
    h = xn * (1.0 + mn[shift_idx + 1:shift_idx + 2]) + mn[shift_idx:shift_idx + 1]
    ho_ref[...] = h.astype(ho_ref.dtype)


def _res_ln(x, y, mod_gate, gate_idx, mod_next, shift_idx, ln_w, ln_b, alpha,
            tiles_per_batch, ctx_tiles, h_dtype, x_skip_tiles=0, x_ctx=None):
    rows, d = y.shape
    ctx_row = mod_gate.shape[0] - 1
    x_tiles = tiles_per_batch + x_skip_tiles
    if x_ctx is None:
        stream = None
        x_ctx = x
        c_spec = pl.BlockSpec((ROW_TILE, d), lambda r: (0, 0))
        x_spec = pl.BlockSpec(
            (ROW_TILE, d),
            lambda r: ((r // tiles_per_batch) * x_tiles + x_skip_tiles + r % tiles_per_batch, 0))
    else:
        assert x_skip_tiles == 0
        stream = (tiles_per_batch, ctx_tiles)
        x_spec, c_spec = _token_source_specs(tiles_per_batch, ctx_tiles, d)
    midx = functools.partial(_mod_index, tiles_per_batch=tiles_per_batch,
                             ctx_tiles=ctx_tiles, ctx_row=ctx_row)
    row_spec = pl.BlockSpec((ROW_TILE, d), lambda r: (r, 0))
    mod_spec = pl.BlockSpec((1, 6, d), lambda r: (midx(r), 0, 0))
    vec_spec = pl.BlockSpec((1, d), lambda r: (0, 0))
    return pl.pallas_call(
        functools.partial(_res_ln_kernel, alpha=alpha, gate_idx=gate_idx, shift_idx=shift_idx,
                          stream=stream),
        out_shape=(jax.ShapeDtypeStruct((rows, d), F32), jax.ShapeDtypeStruct((rows, d), h_dtype)),
        grid=(rows // ROW_TILE,),
        in_specs=[x_spec, c_spec, row_spec, mod_spec, mod_spec, vec_spec, vec_spec],
        out_specs=(row_spec, row_spec),
        compiler_params=_params("parallel"),
        name="residual_layer_norm",
    )(x, x_ctx, y, mod_gate, mod_next, ln_w.reshape(1, d), ln_b.reshape(1, d))


def _mm_kernel(a_ref, w_ref, o_ref, wbf_ref):
    @pl.when(pl.program_id(1) == 0)
    def _():
        wbf_ref[...] = w_ref[...].astype(BF16)

    o_ref[...] = _dot(a_ref[...], wbf_ref[...]).astype(o_ref.dtype)


def _matmul(a, w, *, tm, tn, out_dtype, col_start=0, n_cols=None):
    m, k = a.shape
    tm = _row_tile(m, tm)
    n_cols = w.shape[1] - col_start if n_cols is None else n_cols
    assert m % tm == 0 and n_cols % tn == 0 and col_start % tn == 0
    off = col_start // tn
    return pl.pallas_call(
        _mm_kernel,
        out_shape=jax.ShapeDtypeStruct((m, n_cols), out_dtype),
        grid=(n_cols // tn, m // tm),
        in_specs=[pl.BlockSpec((tm, k), lambda j, i: (i, 0)),
                  pl.BlockSpec((k, tn), lambda j, i: (0, j + off))],
        out_specs=pl.BlockSpec((tm, tn), lambda j, i: (i, j)),
        scratch_shapes=[pltpu.VMEM((k, tn), BF16)],
        compiler_params=_params("parallel", "arbitrary"),
        name="matmul",
    )(a, w)


def _swiglu_up_kernel(a_ref, wg_ref, wu_ref, o_ref, wgb_ref, wub_ref):
    @pl.when(pl.program_id(1) == 0)
    def _():
        wgb_ref[...] = wg_ref[...].astype(BF16)
        wub_ref[...] = wu_ref[...].astype(BF16)

    a = a_ref[...]
    g = _dot(a, wgb_ref[...])
    u = _dot(a, wub_ref[...])
    o_ref[...] = (_silu(g) * u).astype(o_ref.dtype)


def _swiglu_up(a, w_gu, *, tm, tn):
    m, k = a.shape
    tm = _row_tile(m, tm)
    ff = w_gu.shape[1] // 2
    assert m % tm == 0 and ff % tn == 0
    nj = ff // tn
    return pl.pallas_call(
        _swiglu_up_kernel,
        out_shape=jax.ShapeDtypeStruct((m, ff), BF16),
        grid=(nj, m // tm),
        in_specs=[pl.BlockSpec((tm, k), lambda j, i: (i, 0)),
                  pl.BlockSpec((k, tn), lambda j, i: (0, j)),
                  pl.BlockSpec((k, tn), lambda j, i: (0, j + nj))],
        out_specs=pl.BlockSpec((tm, tn), lambda j, i: (i, j)),
        scratch_shapes=[pltpu.VMEM((k, tn), BF16), pltpu.VMEM((k, tn), BF16)],
        compiler_params=_params("parallel", "arbitrary"),
        name="swiglu_up",
    )(a, w_gu, w_gu)


def _gdn_conv_kernel(cur_ref, prev_ref, next_ref, w_ref, o_ref, xe_ref, *, n_qk_blocks, q_scale):
    i = pl.program_id(1)
    j = pl.program_id(2)
    n_tiles = pl.num_programs(1)
    prev_ok = (i >= 2).astype(F32)
    next_ok = jnp.logical_and(i >= 1, i < n_tiles - 1).astype(F32)
    xe_ref[0:8, :] = prev_ref[0].astype(F32)[8:16] * prev_ok
    xe_ref[8:8 + ROW_TILE, :] = cur_ref[0].astype(F32)
    xe_ref[8 + ROW_TILE:16 + ROW_TILE, :] = next_ref[0].astype(F32)[0:8] * next_ok
    pad = (GDN_CONV - 1) // 2
    is_qk = j < n_qk_blocks
    scale = jnp.where(j < n_qk_blocks // 2, q_scale, 1.0)
    for g in range(xe_ref.shape[1] // GDN_DK):
        sl = slice(g * GDN_DK, (g + 1) * GDN_DK)
        xe = xe_ref[:, sl]
        acc = None
        for t in range(GDN_CONV):
            shifted = xe if t == pad else pltpu.roll(xe, (pad - t) % xe.shape[0], axis=0)
            term = shifted[8:8 + ROW_TILE] * w_ref[t:t + 1, sl]
            acc = term if acc is None else acc + term
        y = _silu(acc)
        inv = lax.rsqrt(jnp.sum(y * y, axis=-1, keepdims=True) + 1e-6) * scale
        o_ref[0, :, sl] = (y * jnp.where(is_qk, inv, 1.0)).astype(o_ref.dtype)


def _gdn_conv(pqz, conv_w, batch):
    rows, _ = pqz.shape
    t_all = rows // batch
    x = pqz.reshape(batch, t_all, pqz.shape[1])
    cb = 1024
    n_tiles = t_all // ROW_TILE
    halo = 16
    last_halo = t_all // halo - 1
    per_tile = ROW_TILE // halo
    return pl.pallas_call(
        functools.partial(_gdn_conv_kernel, n_qk_blocks=2 * GDN_Q // cb, q_scale=GDN_DK ** -0.5),
        out_shape=jax.ShapeDtypeStruct((batch, t_all, GDN_QKV), BF16),
        grid=(batch, n_tiles, GDN_QKV // cb),
        in_specs=[pl.BlockSpec((1, ROW_TILE, cb), lambda b, i, j: (b, i, j)),
                  pl.BlockSpec((1, halo, cb), lambda b, i, j: (b, jnp.maximum(i * per_tile - 1, 0), j)),
                  pl.BlockSpec((1, halo, cb),
                               lambda b, i, j: (b, jnp.minimum((i + 1) * per_tile, last_halo), j)),
                  pl.BlockSpec((GDN_CONV, cb), lambda b, i, j: (0, j))],
        out_specs=pl.BlockSpec((1, ROW_TILE, cb), lambda b, i, j: (b, i, j)),
        scratch_shapes=[pltpu.VMEM((ROW_TILE + 16, cb), F32)],
        compiler_params=_params("parallel", "parallel", "parallel"),
        name="gdn_conv",
    )(x, x, x, conv_w)


def _gdn_gate_kernel(ab_ref, alog_ref, dtb_ref, o_ref):
    ab = ab_ref[...]
    half = LANES // 2
    quarter = LANES // 4
    lane = lax.broadcasted_iota(jnp.int32, ab.shape, 1)
    beta = jax.nn.sigmoid(ab)
    x = ab + dtb_ref[...]
    softplus = jnp.maximum(x, 0.0) + jnp.log1p(jnp.exp(-jnp.abs(x)))
    g = jnp.where(lane >= half, -jnp.exp(alog_ref[...]) * softplus, 0.0)
    g = pltpu.roll(g, half, axis=1)
    n = ab.shape[0]
    ri = lax.broadcasted_iota(jnp.int32, (n, n), 0)
    ci = lax.broadcasted_iota(jnp.int32, (n, n), 1)
    same = (ri // GDN_CHUNK) == (ci // GDN_CHUNK)
    hi = lax.Precision.HIGHEST
    m_fwd = jnp.logical_and(same, ci <= ri).astype(F32)
    m_bwd = jnp.logical_and(same, ci >= ri).astype(F32)
    gc_f = jnp.dot(m_fwd, g, preferred_element_type=F32, precision=hi)
    gc_b = jnp.dot(m_bwd, g, preferred_element_type=F32, precision=hi)
    gc = jnp.where(lane < quarter, gc_f, gc_b)
    for d in range(2):
        parts = [pltpu.roll(val, (q * quarter - d * quarter) % LANES, axis=1)
                 for q, val in enumerate((beta, gc))]
        o_ref[d] = jnp.where(lane < quarter, parts[0], jnp.where(lane < 2 * quarter, parts[1], 0.0))


def _gdn_gates(ab, a_log, dt_bias):
    rows = ab.shape[0]
    zeros = jnp.zeros((LANES // 2,), F32)
    alog = jnp.concatenate([zeros, a_log.reshape(-1).astype(F32)]).reshape(1, LANES)
    dtb = jnp.concatenate([zeros, dt_bias.reshape(-1).astype(F32)]).reshape(1, LANES)
    return pl.pallas_call(
        _gdn_gate_kernel,
        out_shape=jax.ShapeDtypeStruct((2, rows, LANES), F32),
        grid=(rows // ROW_TILE,),
        in_specs=[pl.BlockSpec((ROW_TILE, LANES), lambda r: (r, 0)),
                  pl.BlockSpec((1, LANES), lambda r: (0, 0)),
                  pl.BlockSpec((1, LANES), lambda r: (0, 0))],
        out_specs=pl.BlockSpec((2, ROW_TILE, LANES), lambda r: (0, r, 0)),
        compiler_params=_params("parallel"),
        name="gdn_gates",
    )(ab, alog, dtb)


def _block_masks(ri, ci, n):
    masks = [(ri >> 1) == (ci >> 1)]
    for level in range(1, int(math.log2(n))):
        masks.append(jnp.logical_and((ri >> (level + 1)) == (ci >> (level + 1)),
                                     (ri >> level) != (ci >> level)))
    return masks


def _bmm(a, b):
    return jnp.einsum('hij,hjk->hik', a, b, preferred_element_type=F32)


def _bmm_nt(a, b):
    return jnp.einsum('hik,hjk->hij', a, b, preferred_element_type=F32)


def _bmm_tn(a, b):
    return jnp.einsum('hki,hkj->hij', a, b, preferred_element_type=F32)


def _pair_block_diag(x, left):
    return jnp.concatenate([jnp.where(left, x, 0.0), jnp.where(left, 0.0, x)], axis=1).astype(BF16)


def _unit_triangular_inverse(a, eye, masks, left):
    d = eye - jnp.where(masks[0], a, 0.0)
    for mask in masks[1:]:
        ad = _bmm(jnp.where(mask, a, 0.0).astype(BF16), _pair_block_diag(d, left))
        d = d - _bmm(d.astype(BF16), _pair_block_diag(ad, left))
    return d


def _gdn_scan_kernel(q_ref, k_ref, v_ref, g_ref, gr_ref, o_ref, s_ref):
    d = pl.program_id(1)
    s = pl.program_id(2)

    @pl.when(s == 0)
    def _():
        s_ref[...] = jnp.zeros_like(s_ref)

    c = GDN_CHUNK
    n_h = GDN_V_HEADS
    rep = GDN_V_HEADS // GDN_QK_HEADS
    assert rep == 2
    rev = d == 1
    ri = lax.broadcasted_iota(jnp.int32, (c, 2 * c), 0)
    lane = lax.broadcasted_iota(jnp.int32, (c, 2 * c), 1)
    left = lane < c
    ci = lane % c
    later = jnp.where(rev, ci, ri)
    earlier = jnp.where(rev, ri, ci)
    incl = later >= earlier
    strict = later > earlier
    eye = (ri == ci).astype(F32)
    masks = _block_masks(ri, ci, c)

    def split_pairs(x):
        return jnp.stack([x[j, i * c:(i + 1) * c] for j in range(GDN_QK_HEADS) for i in range(rep)])

    def prepare(cc):
        off = pl.multiple_of(cc * c, c)
        q_all = q_ref[0, pl.ds(off, c), :]
        k_all = k_ref[0, pl.ds(off, c), :]
        v_all = v_ref[0, pl.ds(off, c), :]
        gates = g_ref[0, pl.ds(off, c), :]
        gc_rows = gr_ref[0, 0, cc]
        qs = [q_all[:, i * GDN_DK:(i + 1) * GDN_DK] for i in range(GDN_QK_HEADS)]
        ks = [k_all[:, i * GDN_DK:(i + 1) * GDN_DK] for i in range(GDN_QK_HEADS)]
        q_over_k = jnp.stack([jnp.concatenate([qs[i], ks[i]], axis=0) for i in range(GDN_QK_HEADS)])
        k_twice = jnp.stack([jnp.concatenate([ks[i], ks[i]], axis=0) for i in range(GDN_QK_HEADS)])
        qk_kk = _bmm_nt(q_over_k, k_twice)

        def col(quantity, h):
            idx = quantity * n_h + h
            return gates[:, idx:idx + 1]

        a_l, qk_l, rhs_l, qg_l, kt_l, egl_l = [], [], [], [], [], []
        for j in range(GDN_QK_HEADS):
            kf = ks[j].astype(F32)
            qf = qs[j].astype(F32)
            rhs_pair, beta_pair, gc_pair = [], [], []
            for h in (rep * j, rep * j + 1):
                beta = jnp.broadcast_to(col(0, h), (c, 2 * c))
                gc = jnp.broadcast_to(col(1, h), (c, 2 * c))
                g_total = jnp.where(rev, gc[0:1, :], gc[c - 1:c, :])
                eg = jnp.exp(gc)
                ek = jnp.exp(g_total - gc)
                vf = v_all[:, h * GDN_DV:(h + 1) * GDN_DV].astype(F32)
                rhs_pair.append(jnp.concatenate([vf * beta, kf * (beta * eg)], axis=1).astype(BF16))
                qg_l.append((qf * eg).astype(BF16))
                kt_l.append((kf * ek).T.astype(BF16))
                egl_l.append(jnp.exp(g_total))
                beta_pair.append(beta)
                gc_pair.append(gc)
            rhs_l.append(jnp.concatenate(rhs_pair, axis=0))
            beta_p = jnp.where(left, beta_pair[0], beta_pair[1])
            gc_p = jnp.where(left, gc_pair[0], gc_pair[1])
            gdiff = gc_p - gc_rows[j:j + 1, :]
            decay = jnp.where(incl, jnp.exp(jnp.where(incl, gdiff, 0.0)), 0.0)
            a_l.append(jnp.where(strict, qk_kk[j, c:] * (beta_p * decay), 0.0))
            qk_l.append(qk_kk[j, 0:c] * decay)
        return a_l, qk_l, rhs_l, qg_l, kt_l, egl_l

    def chunk_step(position, carry):
        cc = jnp.where(rev, CHUNKS_PER_TILE - 1 - position, position)
        a_v, qk_v, rhs_v, qg_v, kt_v, egl_v = (jnp.stack(parts) for parts in prepare(cc))
        t_inv = _unit_triangular_inverse(a_v, eye, masks, left)
        uw = split_pairs(_bmm(_pair_block_diag(t_inv, left), rhs_v))
        u = uw[:, :, 0:GDN_DV]
        w = uw[:, :, GDN_DV:]
        state = s_ref[...]
        state_b = state.astype(BF16)
        w_qg = _bmm(jnp.concatenate([w.astype(BF16), qg_v], axis=1), state_b)
        v_new = u - w_qg[:, 0:c]
        v_new_b = v_new.astype(BF16)
        v_pairs = jnp.stack([jnp.concatenate([v_new_b[rep * j], v_new_b[rep * j + 1]], axis=0)
                             for j in range(GDN_QK_HEADS)])
        o = w_qg[:, c:] + split_pairs(_bmm(_pair_block_diag(qk_v, left), v_pairs))
        s_ref[...] = state * egl_v + _bmm(kt_v, v_new_b)
        off = pl.multiple_of(cc * c, c)
        for h in range(n_h):
            o_ref[0, 0, pl.ds(off, c), h * GDN_DV:(h + 1) * GDN_DV] = o[h].astype(o_ref.dtype)
        return carry

    lax.fori_loop(0, CHUNKS_PER_TILE, chunk_step, 0, unroll=2)


def _gdn_scan(qkv, gates, gc_rows, batch):
    _, t_all, _ = qkv.shape
    n_tiles = t_all // ROW_TILE

    def tile(d, s):
        return jnp.where(d == 0, s, jnp.where(s == 0, 0, n_tiles - s))

    return pl.pallas_call(
        _gdn_scan_kernel,
        out_shape=jax.ShapeDtypeStruct((2, batch, t_all, GDN_V), BF16),
        grid=(batch, 2, n_tiles),
        in_specs=[
            pl.BlockSpec((1, ROW_TILE, GDN_Q), lambda b, d, s: (b, tile(d, s), 0)),
            pl.BlockSpec((1, ROW_TILE, GDN_Q), lambda b, d, s: (b, tile(d, s), 1)),
            pl.BlockSpec((1, ROW_TILE, GDN_V), lambda b, d, s: (b, tile(d, s), 1)),
            pl.BlockSpec((1, ROW_TILE, LANES), lambda b, d, s: (d, b * n_tiles + tile(d, s), 0)),
            pl.BlockSpec((1, 1, CHUNKS_PER_TILE, GDN_QK_HEADS, 2 * GDN_CHUNK),
                         lambda b, d, s: (b, d, tile(d, s), 0, 0)),
        ],
        out_specs=pl.BlockSpec((1, 1, ROW_TILE, GDN_V), lambda b, d, s: (d, b, tile(d, s), 0)),
        scratch_shapes=[pltpu.VMEM((GDN_V_HEADS, GDN_DK, GDN_DV), F32)],
        compiler_params=_params("parallel", "arbitrary", "arbitrary"),
        name="gdn_scan",
    )(qkv, qkv, qkv, gates, gc_rows)


def _gdn_norm_gate_kernel(of_ref, ob_ref, z_ref, w_ref, o_ref):
    w = w_ref[...]
    for g in range(o_ref.shape[1] // GDN_DV):
        sl = slice(g * GDN_DV, (g + 1) * GDN_DV)
        og = of_ref[0, :, sl].astype(F32) + ob_ref[0, :, sl].astype(F32)
        inv = lax.rsqrt(jnp.mean(og * og, axis=-1, keepdims=True) + RMS_EPS)
        o_ref[:, sl] = (og * inv * w * _silu(z_ref[:, sl].astype(F32))).astype(o_ref.dtype)


def _gdn_norm_gate(o2, pqz, norm_w):
    _, rows, width = o2.shape
    cb = width
    z_off = GDN_QKV // cb
    return pl.pallas_call(
        _gdn_norm_gate_kernel,
        out_shape=jax.ShapeDtypeStruct((rows, width), BF16),
        grid=(rows // ROW_TILE, width // cb),
        in_specs=[pl.BlockSpec((1, ROW_TILE, cb), lambda r, j: (0, r, j)),
                  pl.BlockSpec((1, ROW_TILE, cb), lambda r, j: (1, r, j)),
                  pl.BlockSpec((ROW_TILE, cb), lambda r, j: (r, z_off + j)),
                  pl.BlockSpec((1, GDN_DV), lambda r, j: (0, 0))],
        out_specs=pl.BlockSpec((ROW_TILE, cb), lambda r, j: (r, j)),
        compiler_params=_params("parallel", "parallel"),
        name="gdn_norm_gate",
    )(o2, o2, pqz, norm_w.reshape(1, GDN_DV))


def _gdn_mixer(h, batch, w_in, conv_w, a_log, dt_bias, norm_w, w_out):
    rows = h.shape[0]
    t_all = rows // batch
    pqz = _matmul(h, w_in, tm=1024, tn=1024, out_dtype=BF16, col_start=0, n_cols=GDN_QKV + GDN_V)
    ab = _matmul(h, w_in, tm=1024, tn=LANES, out_dtype=F32, col_start=GDN_QKV + GDN_V, n_cols=LANES)
    gates = _gdn_gates(ab, a_log, dt_bias)
    n_chunks = t_all // GDN_CHUNK
    gc_rows = gates[:, :, GDN_V_HEADS:2 * GDN_V_HEADS]
    gc_rows = gc_rows.reshape(2, batch, n_chunks, GDN_CHUNK, GDN_V_HEADS).transpose(1, 0, 2, 4, 3)
    gc_rows = gc_rows.reshape(batch, 2, n_chunks, GDN_QK_HEADS, 2 * GDN_CHUNK)
    qkv = _gdn_conv(pqz, conv_w, batch)
    o2 = _gdn_scan(qkv, gates, gc_rows, batch)
    og = _gdn_norm_gate(o2.reshape(2, rows, GDN_V), pqz, norm_w)
    return _matmul(og, w_out, tm=1024, tn=512, out_dtype=BF16)


def _rope_tables(n_ctx, n_lat):
    rows = n_lat // GRID_W
    row = jnp.repeat(jnp.arange(rows, dtype=F32), GRID_W)
    col = jnp.tile(jnp.arange(GRID_W, dtype=F32), rows)
    inv_freq = 1.0 / (ROPE_BASE ** (jnp.arange(0, ROPE_AXIS_DIM, 2, dtype=F32) / ROPE_AXIS_DIM))
    ang_r = row[:, None] * inv_freq
    ang_c = col[:, None] * inv_freq
    cos = jnp.concatenate([jnp.cos(ang_r), jnp.cos(ang_r), jnp.cos(ang_c), jnp.cos(ang_c)], axis=1)
    sin = jnp.concatenate([-jnp.sin(ang_r), jnp.sin(ang_r), -jnp.sin(ang_c), jnp.sin(ang_c)], axis=1)
    cos = jnp.concatenate([jnp.ones((n_ctx, DIFF_DH), F32), cos], axis=0)
    sin = jnp.concatenate([jnp.zeros((n_ctx, DIFF_DH), F32), sin], axis=0)
    return cos, sin


def _qkv_rope_kernel(a_ref, w_ref, cos_ref, sin_ref, o_ref, wbf_ref, *, n_rope_blocks, n_q_blocks,
                     q_scale):
    j = pl.program_id(0)

    @pl.when(pl.program_id(1) == 0)
    def _():
        wbf_ref[...] = w_ref[...].astype(BF16)

    acc = _dot(a_ref[...], wbf_ref[...])

    @pl.when(j >= n_rope_blocks)
    def _():
        o_ref[...] = acc.astype(o_ref.dtype)

    @pl.when(j < n_rope_blocks)
    def _():
        scale = jnp.where(j < n_q_blocks, q_scale, 1.0)
        cos = cos_ref[...] * scale
        sin = sin_ref[...] * scale
        quarter = ROPE_AXIS_DIM // 2
        lane = lax.broadcasted_iota(jnp.int32, cos.shape, 1)
        first = (lane % ROPE_AXIS_DIM) < quarter
        for g in range(acc.shape[1] // DIFF_DH):
            sl = slice(g * DIFF_DH, (g + 1) * DIFF_DH)
            xg = acc[:, sl]
            partner = jnp.where(first, pltpu.roll(xg, DIFF_DH - quarter, axis=1),
                                pltpu.roll(xg, quarter, axis=1))
            o_ref[:, sl] = (xg * cos + partner * sin).astype(o_ref.dtype)


def _qkv_rope(a, w, cos, sin, *, tm, tn, q_scale):
    m, k = a.shape
    tm = _row_tile(m, tm)
    n = w.shape[1]
    assert n % tn == 0 and DIFF_Q % tn == 0
    return pl.pallas_call(
        functools.partial(_qkv_rope_kernel, n_rope_blocks=2 * DIFF_Q // tn, n_q_blocks=DIFF_Q // tn,
                          q_scale=q_scale),
        out_shape=jax.ShapeDtypeStruct((m, n), BF16),
        grid=(n // tn, m // tm),
        in_specs=[pl.BlockSpec((tm, k), lambda j, i: (i, 0)),
                  pl.BlockSpec((k, tn), lambda j, i: (0, j)),
                  pl.BlockSpec((tm, DIFF_DH), lambda j, i: (i, 0)),
                  pl.BlockSpec((tm, DIFF_DH), lambda j, i: (i, 0))],
        out_specs=pl.BlockSpec((tm, tn), lambda j, i: (i, j)),
        scratch_shapes=[pltpu.VMEM((k, tn), BF16)],
        compiler_params=_params("parallel", "arbitrary"),
        name="qkv_rope",
    )(a, w, cos, sin)


def _diff_attn_kernel(lam_ref, nw_ref, q_ref, k_ref, v_ref, o_ref, *, lam_init, key_chunk):
    lp = lam_ref[...]
    lam = (jnp.exp(jnp.sum(lp[0:1] * lp[1:2], axis=-1, keepdims=True))
           - jnp.exp(jnp.sum(lp[2:3] * lp[3:4], axis=-1, keepdims=True)) + lam_init)
    q = q_ref[0]
    n_keys = k_ref.shape[1]

    def attend(m):
        sl = slice(m * DIFF_DH, (m + 1) * DIFF_DH)
        qm = q[:, sl]
        m_run = l_run = acc = None
        for c in range(pl.cdiv(n_keys, key_chunk)):
            rows = slice(c * key_chunk, min((c + 1) * key_chunk, n_keys))
            s2 = _dot_nt(qm, k_ref[0, rows, sl])
            m_c = jnp.max(s2, axis=-1, keepdims=True)
            if c == 0:
                m_new = m_c
                p = jnp.exp2(s2 - m_new)
                l_run = jnp.sum(p, axis=-1, keepdims=True)
                acc = _dot(p.astype(BF16), v_ref[0, rows, :])
            else:
                m_new = jnp.maximum(m_run, m_c)
                rescale = jnp.exp2(m_run - m_new)
                p = jnp.exp2(s2 - m_new)
                l_run = l_run * rescale + jnp.sum(p, axis=-1, keepdims=True)
                acc = acc * rescale + _dot(p.astype(BF16), v_ref[0, rows, :])
            m_run = m_new
        return acc, l_run

    o1, l1 = attend(0)
    o2, l2 = attend(1)
    o = o1 * (1.0 / l1) - o2 * (lam / l2)
    inv = lax.rsqrt(jnp.mean(o * o, axis=-1, keepdims=True) + RMS_EPS)
    o_ref[0] = (o * inv * nw_ref[...] * (1.0 - lam_init)).astype(o_ref.dtype)


def _diff_attention(qkv, batch, n_ctx, lam_p, norm_w, lam_init):
    t_all = qkv.shape[0] // batch
    n_lat = t_all - n_ctx
    tq = ROW_TILE
    assert n_ctx % tq == 0
    ctx_tiles = n_ctx // tq
    v3 = qkv.reshape(batch, t_all, qkv.shape[1])
    k_off = DIFF_Q // DIFF_DV
    v_off = 2 * DIFF_Q // DIFF_DV
    return pl.pallas_call(
        functools.partial(_diff_attn_kernel, lam_init=lam_init, key_chunk=2 * ROW_TILE),
        out_shape=jax.ShapeDtypeStruct((batch, n_lat, DIFF_HEADS * DIFF_DV), BF16),
        grid=(batch, DIFF_HEADS, n_lat // tq),
        in_specs=[pl.BlockSpec((4, DIFF_DH), lambda b, h, i: (0, 0)),
                  pl.BlockSpec((1, DIFF_DV), lambda b, h, i: (0, 0)),
                  pl.BlockSpec((1, tq, DIFF_DV), lambda b, h, i: (b, i + ctx_tiles, h)),
                  pl.BlockSpec((1, t_all, DIFF_DV), lambda b, h, i: (b, 0, k_off + h)),
                  pl.BlockSpec((1, t_all, DIFF_DV), lambda b, h, i: (b, 0, v_off + h))],
        out_specs=pl.BlockSpec((1, tq, DIFF_DV), lambda b, h, i: (b, i, h)),
        compiler_params=_params("parallel", "parallel", "arbitrary"),
        name="diff_attention",
    )(lam_p.astype(F32), norm_w.reshape(1, DIFF_DV), v3, v3, v3)


def _router_kernel(h_ref, r_ref, idx_ref, w_ref):
    logits = _dot_nt(r_ref[...].astype(BF16), h_ref[...].astype(BF16))
    e = lax.broadcasted_iota(jnp.int32, logits.shape, 0)
    n_e = logits.shape[0]
    m1 = jnp.max(logits, axis=0, keepdims=True)
    i1 = jnp.min(jnp.where(logits == m1, e, n_e), axis=0, keepdims=True)
    rest = jnp.where(e == i1, -jnp.inf, logits)
    m2 = jnp.max(rest, axis=0, keepdims=True)
    i2 = jnp.min(jnp.where(rest == m2, e, n_e), axis=0, keepdims=True)
    t = jnp.exp(m2 - m1)
    idx_ref[0:1, :] = i1
    idx_ref[1:2, :] = i2
    w_ref[0:1, :] = 1.0 / (1.0 + t)
    w_ref[1:2, :] = t / (1.0 + t)


def _router(h, router):
    rows, d = h.shape
    tm = _row_tile(rows, 1024)
    return pl.pallas_call(
        _router_kernel,
        out_shape=(jax.ShapeDtypeStruct((2, rows), jnp.int32), jax.ShapeDtypeStruct((2, rows), F32)),
        grid=(rows // tm,),
        in_specs=[pl.BlockSpec((tm, d), lambda i: (i, 0)),
                  pl.BlockSpec((N_EXPERTS, d), lambda i: (0, 0))],
        out_specs=(pl.BlockSpec((2, tm), lambda i: (0, i)), pl.BlockSpec((2, tm), lambda i: (0, i))),
        compiler_params=_params("parallel"),
        name="moe_router",
    )(h, router.T)


def _row_copy(src_hbm, dst_vmem, sem, src_row, dst_row):
    return pltpu.make_async_copy(src_hbm.at[pl.ds(src_row, 1)], dst_vmem.at[pl.ds(dst_row, 1)], sem)


def _gather_rows_kernel(idx_ref, src_ref, o_ref, buf_ref, sem):
    n = o_ref.shape[0]
    i = pl.program_id(0)
    slot = i % 2

    def issue_tile(tile, dst_slot):
        def body(r2, carry):
            for p in range(2):
                r = 2 * r2 + p
                _row_copy(src_ref, buf_ref.at[dst_slot], sem.at[dst_slot],
                          idx_ref[tile * n + r], r).start(priority=p)
            return carry
        lax.fori_loop(0, n // 2, body, 0, unroll=4)

    @pl.when(i == 0)
    def _():
        issue_tile(0, 0)

    @pl.when(i + 1 < pl.num_programs(0))
    def _():
        issue_tile(i + 1, 1 - slot)

    def wait_row(r, carry):
        _row_copy(src_ref, buf_ref.at[slot], sem.at[slot], 0, r).wait()
        return carry

    lax.fori_loop(0, n, wait_row, 0, unroll=8)
    o_ref[...] = buf_ref[slot].astype(o_ref.dtype)


def _gather_rows(src, idx, out_dtype):
    n = idx.shape[0]
    d = src.shape[1]
    tg = ROW_TILE
    return pl.pallas_call(
        _gather_rows_kernel,
        out_shape=jax.ShapeDtypeStruct((n, d), out_dtype),
        grid_spec=pltpu.PrefetchScalarGridSpec(
            num_scalar_prefetch=1,
            grid=(n // tg,),
            in_specs=[pl.BlockSpec(memory_space=pl.ANY)],
            out_specs=pl.BlockSpec((tg, d), lambda i, idx: (i, 0)),
            scratch_shapes=[pltpu.VMEM((2, tg, d), src.dtype), pltpu.SemaphoreType.DMA((2,))]),
        compiler_params=_params("arbitrary"),
        name="moe_gather",
    )(idx, src)


def _expert_up_kernel(te_ref, nt_ref, a_ref, wg_ref, wu_ref, o_ref, wgb_ref, wub_ref):
    i = pl.program_id(1)
    new_expert = jnp.logical_or(i == 0, te_ref[i] != te_ref[jnp.maximum(i - 1, 0)])

    @pl.when(new_expert)
    def _():
        wgb_ref[...] = wg_ref[0].astype(BF16)
        wub_ref[...] = wu_ref[0].astype(BF16)

    @pl.when(i < nt_ref[0])
    def _():
        a = a_ref[...]
        g = _dot(a, wgb_ref[...])
        u = _dot(a, wub_ref[...])
        o_ref[...] = (_silu(g) * u).astype(o_ref.dtype)

    @pl.when(i >= nt_ref[0])
    def _():
        o_ref[...] = jnp.zeros_like(o_ref)


def _expert_up(xs, tile_expert, n_tiles_used, w_gu, *, tn):
    p, k = xs.shape
    ff = w_gu.shape[2] // 2
    nj = ff // tn
    tm = MOE_TILE
    return pl.pallas_call(
        _expert_up_kernel,
        out_shape=jax.ShapeDtypeStruct((p, ff), BF16),
        grid_spec=pltpu.PrefetchScalarGridSpec(
            num_scalar_prefetch=2,
            grid=(nj, p // tm),
            in_specs=[pl.BlockSpec((tm, k), lambda j, i, te, nt: (i, 0)),
                      pl.BlockSpec((1, k, tn), lambda j, i, te, nt: (te[i], 0, j)),
                      pl.BlockSpec((1, k, tn), lambda j, i, te, nt: (te[i], 0, j + nj))],
            out_specs=pl.BlockSpec((tm, tn), lambda j, i, te, nt: (i, j)),
            scratch_shapes=[pltpu.VMEM((k, tn), BF16), pltpu.VMEM((k, tn), BF16)]),
        compiler_params=_params("parallel", "arbitrary"),
        name="moe_expert_up",
    )(tile_expert, n_tiles_used, xs, w_gu, w_gu)


def _expert_down_kernel(te_ref, nt_ref, a_ref, w_ref, o_ref, wb_ref):
    i = pl.program_id(1)
    new_expert = jnp.logical_or(i == 0, te_ref[i] != te_ref[jnp.maximum(i - 1, 0)])

    @pl.when(new_expert)
    def _():
        wb_ref[...] = w_ref[0].astype(BF16)

    @pl.when(i < nt_ref[0])
    def _():
        o_ref[...] = _dot(a_ref[...], wb_ref[...]).astype(o_ref.dtype)

    @pl.when(i >= nt_ref[0])
    def _():
        o_ref[...] = jnp.zeros_like(o_ref)


def _expert_down(act, tile_expert, n_tiles_used, w_down, *, tn):
    p, k = act.shape
    n = w_down.shape[2]
    tm = MOE_TILE
    return pl.pallas_call(
        _expert_down_kernel,
        out_shape=jax.ShapeDtypeStruct((p, n), F32),
        grid_spec=pltpu.PrefetchScalarGridSpec(
            num_scalar_prefetch=2,
            grid=(n // tn, p // tm),
            in_specs=[pl.BlockSpec((tm, k), lambda j, i, te, nt: (i, 0)),
                      pl.BlockSpec((1, k, tn), lambda j, i, te, nt: (te[i], 0, j))],
            out_specs=pl.BlockSpec((tm, tn), lambda j, i, te, nt: (i, j)),
            scratch_shapes=[pltpu.VMEM((k, tn), BF16)]),
        compiler_params=_params("parallel", "arbitrary"),
        name="moe_expert_down",
    )(tile_expert, n_tiles_used, act, w_down)


def _combine_ln_kernel(slot_ref, ys_ref, w_ref, x_ref, mg_ref, lnw_ref, lnb_ref, o_ref,
                       buf_ref, sem, *, alpha, gate_idx):
    n = x_ref.shape[0]
    n_tok = n * pl.num_programs(0)
    i = pl.program_id(0)
    slot = i % 2

    def issue_tile(tile, dst_slot):
        def body(r, carry):
            for choice in range(2):
                _row_copy(ys_ref, buf_ref.at[dst_slot, choice], sem.at[dst_slot],
                          slot_ref[choice * n_tok + tile * n + r], r).start(priority=choice)
            return carry
        lax.fori_loop(0, n, body, 0, unroll=4)

    @pl.when(i == 0)
    def _():
        issue_tile(0, 0)

    @pl.when(i + 1 < pl.num_programs(0))
    def _():
        issue_tile(i + 1, 1 - slot)

    def wait_row(r, carry):
        for choice in range(2):
            _row_copy(ys_ref, buf_ref.at[slot, choice], sem.at[slot], 0, r).wait()
        return carry

    lax.fori_loop(0, n, wait_row, 0, unroll=4)
    w = w_ref[...]
    f = w[:, 0:1] * buf_ref[slot, 0] + w[:, 1:2] * buf_ref[slot, 1]
    gate = mg_ref[0][gate_idx:gate_idx + 1]
    t = alpha * x_ref[...] + gate * f
    o_ref[...] = _layer_norm_rows(t, lnw_ref[...], lnb_ref[...])


def _combine_ln(ys, slots, weights, x, mod_gate, gate_idx, ln_w, ln_b, alpha, tiles_per_batch):
    rows, d = x.shape
    tc = ROW_TILE
    row_spec = pl.BlockSpec((tc, d), lambda r, s: (r, 0))
    vec_spec = pl.BlockSpec((1, d), lambda r, s: (0, 0))
    return pl.pallas_call(
        functools.partial(_combine_ln_kernel, alpha=alpha, gate_idx=gate_idx),
        out_shape=jax.ShapeDtypeStruct((rows, d), F32),
        grid_spec=pltpu.PrefetchScalarGridSpec(
            num_scalar_prefetch=1,
            grid=(rows // tc,),
            in_specs=[pl.BlockSpec(memory_space=pl.ANY),
                      pl.BlockSpec((tc, 2), lambda r, s: (r, 0)),
                      row_spec,
                      pl.BlockSpec((1, 6, d), lambda r, s: (r // tiles_per_batch, 0, 0)),
                      vec_spec, vec_spec],
            out_specs=row_spec,
            scratch_shapes=[pltpu.VMEM((2, 2, tc, d), F32), pltpu.SemaphoreType.DMA((2,))]),
        compiler_params=_params("arbitrary"),
        name="moe_combine_layer_norm",
    )(slots, ys, weights, x, mod_gate, ln_w.reshape(1, d), ln_b.reshape(1, d))


def _routing_tables(top_i, n_slots):
    n_tok = top_i.shape[1]
    tm = MOE_TILE
    pair_e = top_i.T.reshape(-1)
    onehot = (pair_e[:, None] == jnp.arange(N_EXPERTS, dtype=jnp.int32)[None, :]).astype(jnp.int32)
    rank = jnp.cumsum(onehot, axis=0) - onehot
    counts = jnp.sum(onehot, axis=0)
    padded = ((counts + tm - 1) // tm) * tm
    ends = jnp.cumsum(padded)
    starts = ends - padded
    slot = jnp.sum(onehot * (starts[None, :] + rank), axis=1)
    token_of_slot = jnp.zeros((n_slots,), jnp.int32).at[slot].set(
        jnp.arange(2 * n_tok, dtype=jnp.int32) // 2)
    tile_start = jnp.arange(n_slots // tm, dtype=jnp.int32) * tm
    tile_expert = jnp.minimum(
        jnp.sum((tile_start[:, None] >= ends[None, :]).astype(jnp.int32), axis=1), N_EXPERTS - 1)
    n_tiles_used = (ends[-1:] // tm).astype(jnp.int32)
    slots = slot.reshape(n_tok, 2).T.reshape(-1)
    return slots, token_of_slot, tile_expert.astype(jnp.int32), n_tiles_used


def _moe(h_f32, router, w_gu, w_down):
    n_tok = h_f32.shape[0]
    top_i, top_w = _router(h_f32, router)
    n_slots = 2 * n_tok + N_EXPERTS * MOE_TILE
    slots, token_of_slot, tile_expert, n_tiles_used = _routing_tables(top_i, n_slots)
    xs = _gather_rows(h_f32, token_of_slot, BF16)
    act = _expert_up(xs, tile_expert, n_tiles_used, w_gu, tn=1024)
    ys = _expert_down(act, tile_expert, n_tiles_used, w_down, tn=512)
    return ys, slots, top_w.T


def kernel(x, c, ctx, c_ctx, ada_w, ada_b, ln_w, ln_b, gdn_w_in, gdn_conv_w, gdn_a_log, gdn_dt_bias,
           gdn_norm_w, gdn_w_out, ffn_w_gu, ffn_w_down, diff_w_in, diff_lambda, diff_norm_w,
           diff_w_out, moe_router, moe_w_gu, moe_w_down):
    batch, n_lat, d = x.shape
    n_ctx = ctx.shape[1]
    depth = ada_w.shape[0]
    assert depth == 2 and n_ctx == ROW_TILE and n_lat % ROW_TILE == 0
    t_all = n_ctx + n_lat
    alpha = (2.0 * depth) ** 0.25
    tiles_all = t_all // ROW_TILE
    tiles_lat = n_lat // ROW_TILE

    n_cond = batch + 1
    cond = jnp.concatenate([c, c_ctx[None, :], jnp.zeros((-n_cond % 8, d), F32)], axis=0)
    mod = _ada_modulation(cond, ada_w, ada_b)[:, :n_cond].reshape(depth, n_cond, 6, d)

    x_lat = x.reshape(batch * n_lat, d)
    x_ctx = ctx.reshape(batch * n_ctx, d)

    h = _modulate(x_lat, x_ctx, mod[0], tiles_all, 1)
    y = _gdn_mixer(h, batch, gdn_w_in[0], gdn_conv_w[0], gdn_a_log[0], gdn_dt_bias[0],
                   gdn_norm_w[0], gdn_w_out[0])
    xa, h = _res_ln(x_lat, y, mod[0], 2, mod[0], 3, ln_w[0, 0], ln_b[0, 0], alpha, tiles_all, 1, BF16,
                    x_ctx=x_ctx)
    act = _swiglu_up(h, ffn_w_gu[0], tm=1024, tn=512)
    f = _matmul(act, ffn_w_down[0], tm=512, tn=512, out_dtype=BF16)
    xa, h = _res_ln(xa, f, mod[0], 5, mod[1], 0, ln_w[0, 1], ln_b[0, 1], alpha, tiles_all, 1, BF16)

    lam_init = 0.8 - 0.6 * math.exp(-0.3 * 1)
    cos, sin = _rope_tables(n_ctx, n_lat)
    qkv = _qkv_rope(h, diff_w_in[0], jnp.tile(cos, (batch, 1)), jnp.tile(sin, (batch, 1)),
                    tm=1024, tn=1024, q_scale=DIFF_DH ** -0.5 * math.log2(math.e))
    o = _diff_attention(qkv, batch, n_ctx, diff_lambda[0], diff_norm_w[0], lam_init)
    y = _matmul(o.reshape(batch * n_lat, DIFF_HEADS * DIFF_DV), diff_w_out[0],
                tm=1024, tn=1024, out_dtype=BF16)
    xl, hf = _res_ln(xa, y, mod[1], 2, mod[1], 3, ln_w[1, 0], ln_b[1, 0], alpha, tiles_lat, 0, F32,
                     x_skip_tiles=tiles_all - tiles_lat)
    ys, slots, top_w = _moe(hf, moe_router[0], moe_w_gu[0], moe_w_down[0])
    out = _combine_ln(ys, slots, top_w, xl, mod[1], 5, ln_w[1, 1], ln_b[1, 1], alpha, tiles_lat)
    return out.reshape(batch, n_lat, d)
```
